```python
import math
import jax, jax.numpy as jnp
from jax import lax
import numpy as np

D_MODEL = 1024
BATCH = 8
SEQ = 2048
DEPTH = 1

CHUNK = 64
Q_BLOCK = 128
NORM_EPS = 1e-6
ROPE_THETA = 10000.0

DA_WIDTH = 512
DA_HEADS = 4
DA_HALF_DIM = DA_WIDTH // (2 * DA_HEADS)
DA_VDIM = 2 * DA_HALF_DIM
DA_SUBLN_EPS = 1e-5

RW_WIDTH = 512
RW_HEAD = 64
RW_HEADS = RW_WIDTH // RW_HEAD
RW_DECAY_LORA = 64
RW_AAA_LORA = 64
RW_GATE_LORA = 128
RW_GN_EPS = RW_HEAD * 1e-5
RW_IN_WIDTH = 3 * RW_WIDTH + RW_DECAY_LORA + RW_AAA_LORA + RW_GATE_LORA
RW_SPLIT_IDX = [RW_WIDTH, 2 * RW_WIDTH, 3 * RW_WIDTH,
                3 * RW_WIDTH + RW_DECAY_LORA,
                3 * RW_WIDTH + RW_DECAY_LORA + RW_AAA_LORA]

DA_IN_WIDTH = 3 * DA_WIDTH
GATE_WIDTH = 2 * D_MODEL
D_IN = DA_IN_WIDTH + RW_IN_WIDTH + GATE_WIDTH

D_FF = 4 * D_MODEL

kernel_name = "hybrid_diffattn_rwkv7_gated_block"


def rms_norm(x, g, eps=NORM_EPS):
    xf = x.astype(jnp.float32)
    y = xf * lax.rsqrt(jnp.mean(xf * xf, axis=-1, keepdims=True) + eps)
    return (y * g.astype(jnp.float32)).astype(x.dtype)


def rope(x, pos):
    d = x.shape[-1]
    inv_freq = ROPE_THETA ** (-jnp.arange(0, d, 2, dtype=jnp.float32) / d)
    ang = pos[:, None] * inv_freq[None, :]
    cos = jnp.cos(ang)[None, :, None, :].astype(x.dtype)
    sin = jnp.sin(ang)[None, :, None, :].astype(x.dtype)
    x1, x2 = x[..., : d // 2], x[..., d // 2:]
    return jnp.concatenate([x1 * cos - x2 * sin, x2 * cos + x1 * sin], axis=-1)


def diff_attention(q, k, v, lq1, lk1, lq2, lk2, subln_g, lambda_init):
    B, S, _ = q.shape
    pos = jnp.arange(S, dtype=jnp.float32)
    q = rope(q.reshape(B, S, 2 * DA_HEADS, DA_HALF_DIM), pos).reshape(B, S, DA_HEADS, 2, DA_HALF_DIM)
    k = rope(k.reshape(B, S, 2 * DA_HEADS, DA_HALF_DIM), pos).reshape(B, S, DA_HEADS, 2, DA_HALF_DIM)
    v = v.reshape(B, S, DA_HEADS, DA_VDIM)
    lam = (jnp.exp(jnp.sum(lq1.astype(jnp.float32) * lk1.astype(jnp.float32)))
           - jnp.exp(jnp.sum(lq2.astype(jnp.float32) * lk2.astype(jnp.float32)))
           + lambda_init)
    nb = S // Q_BLOCK
    qb = q.reshape(B, nb, Q_BLOCK, DA_HEADS, 2, DA_HALF_DIM).transpose(1, 0, 2, 3, 4, 5)
    k_chunk = jnp.arange(S) // CHUNK
    scale = DA_HALF_DIM ** -0.5

    def attend(args):
        q_blk, bi = args
        s = jnp.einsum('bqhcd,bkhcd->bhcqk', q_blk, k,
                       preferred_element_type=jnp.float32) * scale
        q_chunk = (bi * Q_BLOCK + jnp.arange(Q_BLOCK)) // CHUNK
        allowed = k_chunk[None, :] <= q_chunk[:, None]
        s = jnp.where(allowed, s, -jnp.inf)
        p = jax.nn.softmax(s, axis=-1)
        w = p[:, :, 0] - lam * p[:, :, 1]
        return jnp.einsum('bhqk,bkhv->bqhv', w.astype(v.dtype), v)

    o = lax.map(attend, (qb, jnp.arange(nb)))
    o = o.transpose(1, 0, 2, 3, 4).reshape(B, S, DA_HEADS, DA_VDIM)
    o = rms_norm(o, subln_g, DA_SUBLN_EPS) * (1.0 - lambda_init)
    return o.reshape(B, S, DA_WIDTH)


def rwkv7_time_mix(z, mu, w0, w2, a0, a2, g2, k_k, k_a, r_k, ln_g, ln_b):
    B, S, _ = z.shape
    f32 = jnp.float32
    z_prev = jnp.pad(z, ((0, 0), (1, 0), (0, 0)))[:, :-1]
    z = z + (z_prev - z) * mu
    r, k, v, wd, ad, gd = jnp.split(z, RW_SPLIT_IDX, axis=-1)
    w = -jax.nn.softplus(-(w0 + jnp.tanh(wd) @ w2).astype(f32)) - 0.5
    decay = jnp.exp(-jnp.exp(w))
    a = jax.nn.sigmoid(a0 + ad @ a2)
    g = jax.nn.sigmoid(gd) @ g2
    hd = lambda t: t.reshape(B, S, RW_HEADS, RW_HEAD).astype(f32)
    kk = hd(k * k_k)
    kk = kk / jnp.maximum(jnp.sqrt(jnp.sum(kk * kk, axis=-1, keepdims=True)), 1e-12)
    k = k * (1.0 + (a - 1.0) * k_a)
    r_h, k_h, v_h, a_h, w_h = hd(r), hd(k), hd(v), hd(a), hd(decay)
    a_vec = -kk
    b_vec = kk * a_h

    def step(state, inp):
        r_t, w_t, k_t, v_t, a_t, b_t = inp
        sa = jnp.einsum('bhvk,bhk->bhv', state, a_t)
        state = (state * w_t[:, :, None, :] + sa[..., None] * b_t[:, :, None, :]
                 + v_t[..., None] * k_t[:, :, None, :])
        return state, jnp.einsum('bhvk,bhk->bhv', state, r_t)

    tm = lambda t: jnp.moveaxis(t, 1, 0)
    state0 = jnp.zeros((B, RW_HEADS, RW_HEAD, RW_HEAD), f32)
    _, y = lax.scan(step, state0, (tm(r_h), tm(w_h), tm(k_h), tm(v_h), tm(a_vec), tm(b_vec)))
    y = jnp.moveaxis(y, 0, 1)
    mean = jnp.mean(y, axis=-1, keepdims=True)
    var = jnp.mean(jnp.square(y - mean), axis=-1, keepdims=True)
    y = ((y - mean) * lax.rsqrt(var + RW_GN_EPS)).reshape(B, S, RW_WIDTH)
    y = y * ln_g.astype(f32) + ln_b.astype(f32)
    bonus = jnp.sum(r_h * k_h * r_k.astype(f32), axis=-1, keepdims=True) * v_h
    y = (y + bonus.reshape(B, S, RW_WIDTH)) * g.astype(f32)
    return y.astype(z.dtype)


def setup_inputs(seed: int = 0) -> dict:
    key = jax.random.key(seed)
    ks = jax.random.split(key, 32)
    nrm = lambda k, shape, s: jax.random.normal(k, shape, jnp.float32) * s
    L = DEPTH
    return {
        "x": nrm(ks[0], (BATCH, SEQ, D_MODEL), 1.0),
        "norm_mix_g": 1.0 + nrm(ks[1], (L, D_MODEL), 0.02),
        "w_in": nrm(ks[2], (L, D_MODEL, D_IN), D_MODEL ** -0.5),
        "rw_mu": jax.random.uniform(ks[3], (L, RW_IN_WIDTH), jnp.float32, 0.0, 1.0),
        "rw_w0": jax.random.uniform(ks[4], (L, RW_WIDTH), jnp.float32, -6.5, -1.5),
        "rw_w2": nrm(ks[5], (L, RW_DECAY_LORA, RW_WIDTH), 0.1 * RW_DECAY_LORA ** -0.5),
        "rw_a0": nrm(ks[6], (L, RW_WIDTH), 0.1),
        "rw_a2": nrm(ks[7], (L, RW_AAA_LORA, RW_WIDTH), RW_AAA_LORA ** -0.5),
        "rw_g2": nrm(ks[8], (L, RW_GATE_LORA, RW_WIDTH), RW_GATE_LORA ** -0.5),
        "rw_k_k": 0.85 + nrm(ks[9], (L, RW_WIDTH), 0.02),
        "rw_k_a": 1.0 + nrm(ks[10], (L, RW_WIDTH), 0.02),
        "rw_r_k": nrm(ks[11], (L, RW_HEADS, RW_HEAD), 0.1),
        "rw_ln_g": 1.0 + nrm(ks[12], (L, RW_WIDTH), 0.02),
        "rw_ln_b": nrm(ks[13], (L, RW_WIDTH), 0.02),
        "da_lq1": nrm(ks[14], (L, DA_HALF_DIM), 0.1),
        "da_lk1": nrm(ks[15], (L, DA_HALF_DIM), 0.1),
        "da_lq2": nrm(ks[16], (L, DA_HALF_DIM), 0.1),
        "da_lk2": nrm(ks[17], (L, DA_HALF_DIM), 0.1),
        "da_subln_g": 1.0 + nrm(ks[18], (L, DA_VDIM), 0.02),
        "w_branch_a": nrm(ks[19], (L, DA_WIDTH, D_MODEL), DA_WIDTH ** -0.5),
        "w_branch_b": nrm(ks[20], (L, RW_WIDTH, D_MODEL), RW_WIDTH ** -0.5),
        "w_o": nrm(ks[21], (L, D_MODEL, D_MODEL), D_MODEL ** -0.5),
        "norm_ffn_g": 1.0 + nrm(ks[22], (L, D_MODEL), 0.02),
        "w_ff1": nrm(ks[23], (L, D_MODEL, D_FF), D_MODEL ** -0.5),
        "w_ff2": nrm(ks[24], (L, D_FF, D_MODEL), D_FF ** -0.5),
        "norm_final_g": 1.0 + nrm(ks[25], (D_MODEL,), 0.02),
    }


def reference(x, norm_mix_g, w_in, rw_mu, rw_w0, rw_w2, rw_a0, rw_a2, rw_g2, rw_k_k,
              rw_k_a, rw_r_k, rw_ln_g, rw_ln_b, da_lq1, da_lk1, da_lq2, da_lk2,
              da_subln_g, w_branch_a, w_branch_b, w_o, norm_ffn_g, w_ff1, w_ff2,
              norm_final_g):
    for l in range(DEPTH):
        lambda_init = 0.8 - 0.6 * math.exp(-0.3 * l)
        h = rms_norm(x, norm_mix_g[l])
        proj = h @ w_in[l]
        da_in, rw_in, gates = jnp.split(proj, [DA_IN_WIDTH, DA_IN_WIDTH + RW_IN_WIDTH], axis=-1)
        q, k, v = jnp.split(da_in, 3, axis=-1)
        y_a = diff_attention(q, k, v, da_lq1[l], da_lk1[l], da_lq2[l], da_lk2[l],
                             da_subln_g[l], lambda_init)
        y_b = rwkv7_time_mix(rw_in, rw_mu[l], rw_w0[l], rw_w2[l], rw_a0[l], rw_a2[l],
                             rw_g2[l], rw_k_k[l], rw_k_a[l], rw_r_k[l], rw_ln_g[l], rw_ln_b[l])
        gate_a, gate_b = jnp.split(gates, 2, axis=-1)
        merged = (jax.nn.sigmoid(gate_a) * (y_a @ w_branch_a[l])
                  + jax.nn.sigmoid(gate_b) * (y_b @ w_branch_b[l]))
        x = x + merged @ w_o[l]
        h = rms_norm(x, norm_ffn_g[l])
        x = x + jnp.square(jax.nn.relu(h @ w_ff1[l])) @ w_ff2[l]
    return rms_norm(x, norm_final_g)
```

```python
import functools
import math

import jax
import jax.numpy as jnp
from jax import lax
from jax.experimental import pallas as pl
from jax.experimental.pallas import tpu as pltpu

F32 = jnp.float32
BF16 = jnp.bfloat16

D_MODEL = 1024
CHUNK = 64
NORM_EPS = 1e-6
ROPE_THETA = 10000.0
DA_WIDTH = 512
DA_HEADS = 4
DA_HALF_DIM = 64
DA_VDIM = 128
DA_SUBLN_EPS = 1e-5
RW_WIDTH = 512
RW_HEAD = 64
RW_PAIRS = RW_WIDTH // (2 * RW_HEAD)
RW_DECAY_LORA = 64
RW_AAA_LORA = 64
RW_GATE_LORA = 128
RW_GN_EPS = RW_HEAD * 1e-5
RW_IN_WIDTH = 3 * RW_WIDTH + RW_DECAY_LORA + RW_AAA_LORA + RW_GATE_LORA
DA_IN_WIDTH = 3 * DA_WIDTH
GATE_WIDTH = 2 * D_MODEL
D_IN = DA_IN_WIDTH + RW_IN_WIDTH + GATE_WIDTH
D_FF = 4 * D_MODEL
LAMBDA_INIT = 0.8 - 0.6 * math.exp(0.0)

LANES = 128
VMEM_LIMIT_BYTES = 56 * 1024 * 1024

INPROJ_ROWS = 256
ATTN_BLOCK = 256
RW_PREP_ROWS = 128
FFN_ROWS = 256
FFN_CHUNK = 1024


def _dot(a, b):
    return jnp.dot(a, b, preferred_element_type=F32)


def _dot_nt(a, b):
    return lax.dot_general(a, b, (((1,), (1,)), ((), ())), preferred_element_type=F32)


def _split2(x):
    hi = x.astype(BF16)
    lo = (x - hi.astype(F32)).astype(BF16)
    return hi, lo


def _segsum(x, ones_bd):
    hi, lo = _split2(x)
    return _dot(hi, ones_bd) + _dot(lo, ones_bd)


def _resident(shape):
    nd = len(shape)
    return pl.BlockSpec(shape, lambda *_: (0,) * nd, pipeline_mode=pl.Buffered(1))


def _inproj_kernel(x_ref, g_ref, w_ref, cq_ref, sq_ref, ck_ref, sk_ref,
                   q_ref, k_ref, v_ref, rw_ref, gate_ref):
    x = x_ref[...]
    ms = jnp.mean(x * x, axis=-1, keepdims=True)
    h = (x * lax.rsqrt(ms + NORM_EPS) * g_ref[...]).astype(BF16)

    def proj(c0, c1):
        return _dot(h, w_ref[:, c0:c1])

    rows = x.shape[0]
    lane = lax.broadcasted_iota(jnp.int32, (rows, DA_WIDTH), 1)
    first_half = (lane & (DA_HALF_DIM - 1)) < (DA_HALF_DIM // 2)

    def rope(t, cos, sin):
        partner = jnp.where(first_half,
                            pltpu.roll(t, DA_WIDTH - DA_HALF_DIM // 2, 1),
                            pltpu.roll(t, DA_HALF_DIM // 2, 1))
        return t * cos + partner * sin

    q_ref[...] = rope(proj(0, DA_WIDTH), cq_ref[...], sq_ref[...]).astype(q_ref.dtype)
    k_ref[...] = rope(proj(DA_WIDTH, 2 * DA_WIDTH), ck_ref[...], sk_ref[...]).astype(k_ref.dtype)
    v_ref[...] = proj(2 * DA_WIDTH, 3 * DA_WIDTH).astype(v_ref.dtype)
    rw_ref[...] = proj(DA_IN_WIDTH, DA_IN_WIDTH + RW_IN_WIDTH)
    gate_ref[...] = proj(DA_IN_WIDTH + RW_IN_WIDTH, D_IN)


def _inproj(x2d, g, w_bf, cq, sq, ck, sk, seq):
    m = x2d.shape[0]
    tm = INPROJ_ROWS
    nseq = seq // tm
    row = lambda i: (i, 0)
    tab = lambda i: (i % nseq, 0)
    return pl.pallas_call(
        _inproj_kernel,
        grid=(m // tm,),
        in_specs=[
            pl.BlockSpec((tm, D_MODEL), row),
            _resident((1, D_MODEL)),
            _resident((D_MODEL, D_IN)),
            pl.BlockSpec((tm, DA_WIDTH), tab),
            pl.BlockSpec((tm, DA_WIDTH), tab),
            pl.BlockSpec((tm, DA_WIDTH), tab),
            pl.BlockSpec((tm, DA_WIDTH), tab),
        ],
        out_specs=[
            pl.BlockSpec((tm, DA_WIDTH), row),
            pl.BlockSpec((tm, DA_WIDTH), row),
            pl.BlockSpec((tm, DA_WIDTH), row),
            pl.BlockSpec((tm, RW_IN_WIDTH), row),
            pl.BlockSpec((tm, GATE_WIDTH), row),
        ],
        out_shape=[
            jax.ShapeDtypeStruct((m, DA_WIDTH), BF16),
            jax.ShapeDtypeStruct((m, DA_WIDTH), BF16),
            jax.ShapeDtypeStruct((m, DA_WIDTH), BF16),
            jax.ShapeDtypeStruct((m, RW_IN_WIDTH), F32),
            jax.ShapeDtypeStruct((m, GATE_WIDTH), F32),
        ],
        compiler_params=pltpu.CompilerParams(
            dimension_semantics=("parallel",), vmem_limit_bytes=VMEM_LIMIT_BYTES),
        name="inproj",
    )(x2d, g, w_bf, cq, sq, ck, sk)


def _attn_kernel(lq1_ref, lk1_ref, lq2_ref, lk2_ref, sg_ref, q_ref, k_ref, v_ref,
                 o_ref, m_sc, l_sc, acc_sc):
    tq = q_ref.shape[0]
    i = pl.program_id(2)
    q = q_ref[...]
    lane = lax.broadcasted_iota(jnp.int32, q.shape, 1)
    zero = jnp.zeros_like(q)
    qq = jnp.concatenate([jnp.where(lane < DA_HALF_DIM, q, zero),
                          jnp.where(lane >= DA_HALF_DIM, q, zero)], axis=0)

    m_sc[...] = jnp.full(m_sc.shape, -jnp.inf, F32)
    l_sc[...] = jnp.zeros(l_sc.shape, F32)
    acc_sc[...] = jnp.zeros(acc_sc.shape, F32)

    def step(j, diagonal):
        start = pl.multiple_of(j * tq, tq)
        kb = k_ref[pl.ds(start, tq), :]
        vb = v_ref[pl.ds(start, tq), :]
        s = _dot_nt(qq, kb)
        if diagonal:
            r = lax.broadcasted_iota(jnp.int32, s.shape, 0)
            c = lax.broadcasted_iota(jnp.int32, s.shape, 1)
            allowed = (c // CHUNK) <= ((r & (tq - 1)) // CHUNK)
            s = jnp.where(allowed, s, -jnp.inf)
        m_old = m_sc[...]
        m_new = jnp.maximum(m_old, jnp.max(s, axis=-1, keepdims=True))
        alpha = jnp.exp(m_old - m_new)
        p = jnp.exp(s - m_new)
        l_sc[...] = alpha * l_sc[...] + jnp.sum(p, axis=-1, keepdims=True)
        acc_sc[...] = alpha * acc_sc[...] + _dot(p.astype(BF16), vb)
        m_sc[...] = m_new

    def body(j, carry):
        step(j, False)
        return carry

    lax.fori_loop(0, i, body, 0)
    step(i, True)

    o = acc_sc[...] / l_sc[...]
    lam = (jnp.exp(jnp.sum(lq1_ref[...] * lk1_ref[...], axis=-1, keepdims=True))
           - jnp.exp(jnp.sum(lq2_ref[...] * lk2_ref[...], axis=-1, keepdims=True))
           + LAMBDA_INIT)
    od = o[:tq] - lam * o[tq:]
    ms = jnp.mean(od * od, axis=-1, keepdims=True)
    y = od * lax.rsqrt(ms + DA_SUBLN_EPS) * sg_ref[...]
    o_ref[...] = (y * (1.0 - LAMBDA_INIT)).astype(o_ref.dtype)


def _attention(q, k, v, lq1, lk1, lq2, lk2, subln_g):
    b, s, _ = q.shape
    tq = ATTN_BLOCK
    assert tq & (tq - 1) == 0 and s % tq == 0
    vec = lambda n: pl.BlockSpec((1, n), lambda bi, hi, qi: (0, 0))
    return pl.pallas_call(
        _attn_kernel,
        grid=(b, DA_HEADS, s // tq),
        in_specs=[
            vec(DA_HALF_DIM), vec(DA_HALF_DIM), vec(DA_HALF_DIM), vec(DA_HALF_DIM),
            vec(DA_VDIM),
            pl.BlockSpec((None, tq, DA_VDIM), lambda bi, hi, qi: (bi, qi, hi)),
            pl.BlockSpec((None, s, DA_VDIM), lambda bi, hi, qi: (bi, 0, hi)),
            pl.BlockSpec((None, s, DA_VDIM), lambda bi, hi, qi: (bi, 0, hi)),
        ],
        out_specs=pl.BlockSpec((None, tq, DA_VDIM), lambda bi, hi, qi: (bi, qi, hi)),
        out_shape=jax.ShapeDtypeStruct((b, s, DA_WIDTH), BF16),
        scratch_shapes=[
            pltpu.VMEM((2 * tq, 1), F32),
            pltpu.VMEM((2 * tq, 1), F32),
            pltpu.VMEM((2 * tq, DA_VDIM), F32),
        ],
        compiler_params=pltpu.CompilerParams(
            dimension_semantics=("parallel", "parallel", "arbitrary"),
            vmem_limit_bytes=VMEM_LIMIT_BYTES),
        name="diff_attn",
    )(lq1, lk1, lq2, lk2, subln_g, q, k, v)


def _rwkv_prep_kernel(rw_ref, prev_ref, mu_ref, w0_ref, a0_ref, kk_ref, ka_ref, rk_ref,
                      wa2_ref, g2_ref, ones_ref, tri_ref,
                      rm_ref, yg_ref, bonus_ref, g_ref):
    i = pl.program_id(1)
    tt = rw_ref.shape[0]
    z = rw_ref[...]
    prev = jnp.where(i > 0, prev_ref[7:8, :], 0.0)
    row = lax.broadcasted_iota(jnp.int32, z.shape, 0)
    z_prev = jnp.where(row == 0, prev, pltpu.roll(z, 1, 0))
    zs = z + (z_prev - z) * mu_ref[...]

    w3 = 3 * RW_WIDTH
    r = zs[:, 0:RW_WIDTH]
    k = zs[:, RW_WIDTH:2 * RW_WIDTH]
    v = zs[:, 2 * RW_WIDTH:w3]
    x_wa = zs[:, w3:w3 + LANES]
    x_g = zs[:, w3 + LANES:w3 + 2 * LANES]
    lane = lax.broadcasted_iota(jnp.int32, x_wa.shape, 1)
    t_wa = jnp.where(lane < RW_DECAY_LORA, jnp.tanh(x_wa), x_wa)
    lora = _dot(t_wa.astype(BF16), wa2_ref[...])
    u = -(w0_ref[...] + lora[:, :RW_WIDTH])
    softplus = jnp.maximum(u, 0.0) + jnp.log1p(jnp.exp(-jnp.abs(u)))
    log_decay = -jnp.exp(-softplus - 0.5)
    a_lr = jax.nn.sigmoid(a0_ref[...] + lora[:, RW_WIDTH:])
    g_ref[...] = _dot(jax.nn.sigmoid(x_g).astype(BF16), g2_ref[...])

    ones_bd = ones_ref[...]
    kk = k * kk_ref[...]
    kk = kk / jnp.maximum(jnp.sqrt(_segsum(kk * kk, ones_bd)), 1e-12)
    k2 = k * (1.0 + (a_lr - 1.0) * ka_ref[...])
    a_vec = -kk
    b_vec = kk * a_lr
    bonus_ref[...] = _segsum(r * k2 * rk_ref[...], ones_bd) * v

    tri = tri_ref[...]
    c64 = (CHUNK, LANES)
    t_idx = lax.broadcasted_iota(jnp.int32, c64, 0)
    l_idx = lax.broadcasted_iota(jnp.int32, c64, 1)
    s_idx = l_idx & (RW_HEAD - 1)
    strict = s_idx < t_idx
    incl = s_idx <= t_idx
    diag = s_idx == t_idx
    head0 = l_idx < RW_HEAD
    r128 = lax.broadcasted_iota(jnp.int32, (2 * CHUNK, LANES), 0)
    l128 = lax.broadcasted_iota(jnp.int32, (2 * CHUNK, LANES), 1)
    bd_mask = (r128 < CHUNK) == (l128 < RW_HEAD)
    eye = (r128 == l128).astype(F32)
    zeros_tall = jnp.zeros((2 * CHUNK, LANES), F32)
    zeros_c = jnp.zeros(c64, F32)

    def bd(x):
        return jnp.where(bd_mask, jnp.concatenate([x, x], axis=0), 0.0)

    for c in range(tt // CHUNK):
        rows = slice(c * CHUNK, (c + 1) * CHUNK)
        lw = log_decay[rows]
        h1 = lw.astype(BF16)
        r1 = lw - h1.astype(F32)
        h2 = r1.astype(BF16)
        h3 = (r1 - h2.astype(F32)).astype(BF16)
        cs = _dot(tri, h1) + _dot(tri, h2) + _dot(tri, h3)
        cl = cs[CHUNK - 1:CHUNK, :]
        gam = jnp.exp(cs)
        gam_prev = jnp.exp(cs - lw)
        gam_inv = jnp.exp(-cs)
        gam_end = jnp.exp(cl - cs)
        gam_c = jnp.exp(cl)
        r_t = r[rows] * gam
        a_t = a_vec[rows] * gam_prev
        k_t = k2[rows] * gam_inv
        b_t = b_vec[rows] * gam_inv
        k_h = k2[rows] * gam_end
        b_h = b_vec[rows] * gam_end
        v_c = v[rows]

        for p in range(RW_PAIRS):
            lanes = slice(p * LANES, (p + 1) * LANES)
            r_p, a_p, k_p, b_p = r_t[:, lanes], a_t[:, lanes], k_t[:, lanes], b_t[:, lanes]
            kh_p, bh_p, v_p = k_h[:, lanes], b_h[:, lanes], v_c[:, lanes]

            lhs = jnp.concatenate([a_p, r_p], axis=0).astype(BF16)
            rhs = jnp.concatenate([bd(b_p), bd(k_p)], axis=0).astype(BF16)
            a2 = _dot_nt(lhs, rhs)
            a_ab = jnp.where(strict, a2[:CHUNK, :LANES], 0.0)
            a_ak = jnp.where(strict, a2[:CHUNK, LANES:], 0.0)
            a_rb = jnp.where(incl, a2[CHUNK:, :LANES], 0.0)
            a_rk = jnp.where(incl, a2[CHUNK:, LANES:], 0.0)

            n_bd = bd(a_ab)
            t_mat = eye + n_bd
            n_bf = n_bd.astype(BF16)
            p_mat = _dot(n_bf, n_bf)
            for _ in range(4):
                zz = _dot(p_mat.astype(BF16),
                          jnp.concatenate([p_mat, t_mat], axis=1).astype(BF16))
                p_mat = zz[:, :LANES]
                t_mat = t_mat + zz[:, LANES:]
            t_mat = t_mat + _dot(p_mat.astype(BF16), t_mat.astype(BF16))

            bd_v = bd(v_p)
            av = _dot(a_ak.astype(BF16), bd_v.astype(BF16))
            x_bd = jnp.concatenate([bd(a_p), bd(av)], axis=1)
            tx = _dot(t_mat.astype(BF16), x_bd.astype(BF16))

            ry_rhs = jnp.concatenate(
                [tx, jnp.concatenate([zeros_tall, bd_v], axis=1)], axis=0)
            ry = _dot(jnp.concatenate([a_rb, a_rk], axis=1).astype(BF16), ry_rhs.astype(BF16))
            r_bar = r_p + ry[:, :LANES]
            y_bar = ry[:, LANES:]

            au = tx[:CHUNK] + tx[CHUNK:]
            mg_rhs = jnp.concatenate(
                [au, jnp.concatenate([zeros_c, v_p], axis=1)], axis=0)
            mg_lhs_t = jnp.concatenate([bh_p, kh_p], axis=0).T
            mg = _dot(mg_lhs_t.astype(BF16), mg_rhs.astype(BF16))
            m_pair = (jnp.where(head0, mg[:CHUNK, :LANES], mg[CHUNK:, :LANES])
                      + jnp.where(diag, gam_c[:, lanes], 0.0))
            g_pair = jnp.where(head0, mg[:CHUNK, LANES:], mg[CHUNK:, LANES:])

            rm_ref[c, :, lanes] = jnp.concatenate([r_bar, m_pair], axis=0).astype(rm_ref.dtype)
            yg_ref[c, :, lanes] = jnp.concatenate([y_bar, g_pair], axis=0)


def _rwkv_prep(rw, mu, w0, a0, k_k, k_a, r_k, wa2, g2, ones_bd, tri):
    b, s, _ = rw.shape
    tt = RW_PREP_ROWS
    cpt = tt // CHUNK
    nc = s // CHUNK
    vec = lambda n: pl.BlockSpec((1, n), lambda bi, ti: (0, 0))
    return pl.pallas_call(
        _rwkv_prep_kernel,
        grid=(b, s // tt),
        in_specs=[
            pl.BlockSpec((None, tt, RW_IN_WIDTH), lambda bi, ti: (bi, ti, 0)),
            pl.BlockSpec((None, 8, RW_IN_WIDTH),
                         lambda bi, ti: (bi, jnp.maximum(ti * (tt // 8) - 1, 0), 0)),
            vec(RW_IN_WIDTH), vec(RW_WIDTH), vec(RW_WIDTH), vec(RW_WIDTH), vec(RW_WIDTH),
            vec(RW_WIDTH),
            pl.BlockSpec((LANES, 2 * RW_WIDTH), lambda bi, ti: (0, 0)),
            pl.BlockSpec((RW_GATE_LORA, RW_WIDTH), lambda bi, ti: (0, 0)),
            pl.BlockSpec((RW_WIDTH, RW_WIDTH), lambda bi, ti: (0, 0)),
            pl.BlockSpec((CHUNK, CHUNK), lambda bi, ti: (0, 0)),
        ],
        out_specs=[
            pl.BlockSpec((None, cpt, 2 * CHUNK, RW_WIDTH), lambda bi, ti: (bi, ti, 0, 0)),
            pl.BlockSpec((None, cpt, 2 * CHUNK, RW_WIDTH), lambda bi, ti: (bi, ti, 0, 0)),
            pl.BlockSpec((None, tt, RW_WIDTH), lambda bi, ti: (bi, ti, 0)),
            pl.BlockSpec((None, tt, RW_WIDTH), lambda bi, ti: (bi, ti, 0)),
        ],
        out_shape=[
            jax.ShapeDtypeStruct((b, nc, 2 * CHUNK, RW_WIDTH), BF16),
            jax.ShapeDtypeStruct((b, nc, 2 * CHUNK, RW_WIDTH), F32),
            jax.ShapeDtypeStruct((b, s, RW_WIDTH), F32),
            jax.ShapeDtypeStruct((b, s, RW_WIDTH), F32),
        ],
        compiler_params=pltpu.CompilerParams(
            dimension_semantics=("parallel", "parallel"), vmem_limit_bytes=VMEM_LIMIT_BYTES),
        name="rwkv_prep",
    )(rw, rw, mu, w0, a0, k_k, k_a, r_k, wa2, g2, ones_bd, tri)


def _rwkv_scan_kernel(rm_ref, yg_ref, bonus_ref, g_ref, lng_ref, lnb_ref, ones_ref,
                      y_ref, h_sc, y_sc):
    c = pl.program_id(0)
    nb = rm_ref.shape[0]

    @pl.when(c == 0)
    def _():
        h_sc[...] = jnp.zeros(h_sc.shape, F32)

    r128 = lax.broadcasted_iota(jnp.int32, (2 * CHUNK, LANES), 0)
    l128 = lax.broadcasted_iota(jnp.int32, (2 * CHUNK, LANES), 1)
    bd_mask = (r128 < CHUNK) == (l128 < RW_HEAD)

    def body(bi, carry):
        for p in range(RW_PAIRS):
            lanes = slice(p * LANES, (p + 1) * LANES)
            h_bd = h_sc[bi * RW_PAIRS + p]
            out = _dot(rm_ref[bi, :, lanes], h_bd.astype(BF16)) + yg_ref[bi, :, lanes]
            y_sc[bi, :, lanes] = out[:CHUNK]
            h_new = out[CHUNK:]
            h_sc[bi * RW_PAIRS + p] = jnp.where(
                bd_mask, jnp.concatenate([h_new, h_new], axis=0), 0.0)
        return carry

    lax.fori_loop(0, nb, body, 0)

    ones_bd = ones_ref[...]
    y = y_sc[...].reshape(nb * CHUNK, RW_WIDTH)
    mean = _segsum(y, ones_bd) * (1.0 / RW_HEAD)
    d = y - mean
    var = _segsum(d * d, ones_bd) * (1.0 / RW_HEAD)
    yn = d * lax.rsqrt(var + RW_GN_EPS) * lng_ref[...] + lnb_ref[...]
    bonus = bonus_ref[...].reshape(nb * CHUNK, RW_WIDTH)
    gate = g_ref[...].reshape(nb * CHUNK, RW_WIDTH)
    y_ref[...] = ((yn + bonus) * gate).reshape(nb, CHUNK, RW_WIDTH).astype(y_ref.dtype)


def _rwkv_scan(rm, yg, bonus, g, ln_g, ln_b, ones_bd):
    b, nc, _, _ = rm.shape
    s = nc * CHUNK
    vec = lambda n: pl.BlockSpec((1, n), lambda ci: (0, 0))
    return pl.pallas_call(
        _rwkv_scan_kernel,
        grid=(nc,),
        in_specs=[
            pl.BlockSpec((b, None, 2 * CHUNK, RW_WIDTH), lambda ci: (0, ci, 0, 0)),
            pl.BlockSpec((b, None, 2 * CHUNK, RW_WIDTH), lambda ci: (0, ci, 0, 0)),
            pl.BlockSpec((b, CHUNK, RW_WIDTH), lambda ci: (0, ci, 0)),
            pl.BlockSpec((b, CHUNK, RW_WIDTH), lambda ci: (0, ci, 0)),
            vec(RW_WIDTH), vec(RW_WIDTH),
            pl.BlockSpec((RW_WIDTH, RW_WIDTH), lambda ci: (0, 0)),
        ],
        out_specs=pl.BlockSpec((b, CHUNK, RW_WIDTH), lambda ci: (0, ci, 0)),
        out_shape=jax.ShapeDtypeStruct((b, s, RW_WIDTH), BF16),
        scratch_shapes=[
            pltpu.VMEM((b * RW_PAIRS, 2 * CHUNK, LANES), F32),
            pltpu.VMEM((b, CHUNK, RW_WIDTH), F32),
        ],
        compiler_params=pltpu.CompilerParams(
            dimension_semantics=("arbitrary",), vmem_limit_bytes=VMEM_LIMIT_BYTES),
        name="rwkv_scan",
    )(rm, yg, bonus, g, ln_g, ln_b, ones_bd)


def _merge_ffn_kernel(x_ref, ya_ref, yb_ref, gate_ref, wa_ref, wb_ref, wo_ref,
                      gffn_ref, w1_ref, w2_ref, gfin_ref, o_ref):
    pa = _dot(ya_ref[...], wa_ref[...])
    pb = _dot(yb_ref[...], wb_ref[...])
    gate = gate_ref[...]
    merged = (jax.nn.sigmoid(gate[:, :D_MODEL]) * pa
              + jax.nn.sigmoid(gate[:, D_MODEL:]) * pb)
    x1 = x_ref[...] + _dot(merged.astype(BF16), wo_ref[...])
    ms = jnp.mean(x1 * x1, axis=-1, keepdims=True)
    h = (x1 * lax.rsqrt(ms + NORM_EPS) * gffn_ref[...]).astype(BF16)
    acc = x1
    for c in range(D_FF // FFN_CHUNK):
        cols = slice(c * FFN_CHUNK, (c + 1) * FFN_CHUNK)
        f = jnp.maximum(_dot(h, w1_ref[:, cols]), 0.0)
        acc = acc + _dot((f * f).astype(BF16), w2_ref[cols, :])
    ms2 = jnp.mean(acc * acc, axis=-1, keepdims=True)
    o_ref[...] = acc * lax.rsqrt(ms2 + NORM_EPS) * gfin_ref[...]


def _merge_ffn(x2d, ya, yb, gates, wa, wb, wo, gffn, w1, w2, gfin):
    m = x2d.shape[0]
    tm = FFN_ROWS
    row = lambda i: (i, 0)
    return pl.pallas_call(
        _merge_ffn_kernel,
        grid=(m // tm,),
        in_specs=[
            pl.BlockSpec((tm, D_MODEL), row),
            pl.BlockSpec((tm, DA_WIDTH), row),
            pl.BlockSpec((tm, RW_WIDTH), row),
            pl.BlockSpec((tm, GATE_WIDTH), row),
            _resident((DA_WIDTH, D_MODEL)),
            _resident((RW_WIDTH, D_MODEL)),
            _resident((D_MODEL, D_MODEL)),
            _resident((1, D_MODEL)),
            _resident((D_MODEL, D_FF)),
            _resident((D_FF, D_MODEL)),
            _resident((1, D_MODEL)),
        ],
        out_specs=pl.BlockSpec((tm, D_MODEL), row),
        out_shape=jax.ShapeDtypeStruct((m, D_MODEL), F32),
        compiler_params=pltpu.CompilerParams(
            dimension_semantics=("parallel",), vmem_limit_bytes=VMEM_LIMIT_BYTES),
        name="merge_ffn",
    )(x2d, ya, yb, gates, wa, wb, wo, gffn, w1, w2, gfin)


def _rope_tables(seq):
    d = DA_HALF_DIM
    pos = jnp.arange(seq, dtype=F32)
    inv_freq = ROPE_THETA ** (-jnp.arange(0, d, 2, dtype=F32) / d)
    ang = pos[:, None] * inv_freq[None, :]
    cos = jnp.cos(ang)
    sin = jnp.sin(ang)
    reps = DA_WIDTH // d
    cos_full = jnp.tile(jnp.concatenate([cos, cos], axis=-1), (1, reps))
    sin_full = jnp.tile(jnp.concatenate([-sin, sin], axis=-1), (1, reps))
    return cos_full, sin_full


def kernel(x, norm_mix_g, w_in, rw_mu, rw_w0, rw_w2, rw_a0, rw_a2, rw_g2, rw_k_k, rw_k_a,
           rw_r_k, rw_ln_g, rw_ln_b, da_lq1, da_lk1, da_lq2, da_lk2, da_subln_g,
           w_branch_a, w_branch_b, w_o, norm_ffn_g, w_ff1, w_ff2, norm_final_g):
    b, s, d = x.shape
    assert d == D_MODEL and norm_mix_g.shape[0] == 1
    assert s % ATTN_BLOCK == 0 and s % INPROJ_ROWS == 0 and s % RW_PREP_ROWS == 0
    assert (b * s) % FFN_ROWS == 0
    row = lambda t: t.reshape(1, -1)
    x2d = x.reshape(b * s, d)

    cos, sin = _rope_tables(s)
    scale = DA_HALF_DIM ** -0.5
    q, k, v, rw, gates = _inproj(x2d, norm_mix_g, w_in[0].astype(BF16),
                                 cos * scale, sin * scale, cos, sin, s)

    ya = _attention(q.reshape(b, s, DA_WIDTH), k.reshape(b, s, DA_WIDTH),
                    v.reshape(b, s, DA_WIDTH), row(da_lq1), row(da_lk1), row(da_lq2),
                    row(da_lk2), row(da_subln_g))

    wa2 = jnp.zeros((LANES, 2 * RW_WIDTH), F32)
    wa2 = wa2.at[:RW_DECAY_LORA, :RW_WIDTH].set(rw_w2[0])
    wa2 = wa2.at[RW_DECAY_LORA:, RW_WIDTH:].set(rw_a2[0])
    idx = jnp.arange(RW_WIDTH) // RW_HEAD
    ones_bd = (idx[:, None] == idx[None, :]).astype(BF16)
    tri = (jnp.arange(CHUNK)[:, None] >= jnp.arange(CHUNK)[None, :]).astype(BF16)

    rm, yg, bonus, g = _rwkv_prep(
        rw.reshape(b, s, RW_IN_WIDTH), rw_mu, rw_w0, rw_a0, rw_k_k, rw_k_a, row(rw_r_k),
        wa2.astype(BF16), rw_g2[0].astype(BF16), ones_bd, tri)
    yb = _rwkv_scan(rm, yg, bonus, g, rw_ln_g, rw_ln_b, ones_bd)

    out = _merge_ffn(x2d, ya.reshape(b * s, DA_WIDTH), yb.reshape(b * s, RW_WIDTH), gates,
                     w_branch_a[0].astype(BF16), w_branch_b[0].astype(BF16),
                     w_o[0].astype(BF16), norm_ffn_g, w_ff1[0].astype(BF16),
                     w_ff2[0].astype(BF16), row(norm_final_g))
    return out.reshape(b, s, d)
```

```python
import functools
import math

import jax
import jax.numpy as jnp
from jax import lax
from jax.experimental import pallas as pl
from jax.experimental.pallas import tpu as pltpu

F32 = jnp.float32
BF16 = jnp.bfloat16

D_MODEL = 1024
CHUNK = 64
NORM_EPS = 1e-6
ROPE_THETA = 10000.0
DA_WIDTH = 512
DA_HEADS = 4
DA_HALF_DIM = 64
DA_VDIM = 128
DA_SUBLN_EPS = 1e-5
RW_WIDTH = 512
RW_HEAD = 64
RW_PAIRS = RW_WIDTH // (2 * RW_HEAD)
RW_DECAY_LORA = 64
RW_AAA_LORA = 64
RW_GATE_LORA = 128
RW_GN_EPS = RW_HEAD * 1e-5
RW_IN_WIDTH = 3 * RW_WIDTH + RW_DECAY_LORA + RW_AAA_LORA + RW_GATE_LORA
DA_IN_WIDTH = 3 * DA_WIDTH
GATE_WIDTH = 2 * D_MODEL
D_IN = DA_IN_WIDTH + RW_IN_WIDTH + GATE_WIDTH
D_FF = 4 * D_MODEL
LAMBDA_INIT = 0.8 - 0.6 * math.exp(0.0)

LANES = 128
VMEM_LIMIT_BYTES = 56 * 1024 * 1024

INPROJ_ROWS = 256
ATTN_BLOCK = 256
RW_PREP_ROWS = 256
FFN_ROWS = 256
FFN_CHUNK = 1024


def _dot(a, b):
    return jnp.dot(a, b, preferred_element_type=F32)


def _dot_nt(a, b):
    return lax.dot_general(a, b, (((1,), (1,)), ((), ())), preferred_element_type=F32)


def _split2(x):
    hi = x.astype(BF16)
    lo = (x - hi.astype(F32)).astype(BF16)
    return hi, lo


def _segsum(x, ones_bd):
    hi, lo = _split2(x)
    return _dot(hi, ones_bd) + _dot(lo, ones_bd)


def _resident(shape):
    nd = len(shape)
    return pl.BlockSpec(shape, lambda *_: (0,) * nd, pipeline_mode=pl.Buffered(1))


def _inproj_kernel(x_ref, g_ref, w_ref, cq_ref, sq_ref, ck_ref, sk_ref,
                   q_ref, k_ref, v_ref, rw_ref, gate_ref):
    x = x_ref[...]
    ms = jnp.mean(x * x, axis=-1, keepdims=True)
    h = (x * lax.rsqrt(ms + NORM_EPS) * g_ref[...]).astype(BF16)

    def proj(c0, c1):
        return _dot(h, w_ref[:, c0:c1])

    rows = x.shape[0]
    lane = lax.broadcasted_iota(jnp.int32, (rows, DA_WIDTH), 1)
    first_half = (lane & (DA_HALF_DIM - 1)) < (DA_HALF_DIM // 2)

    def rope(t, cos, sin):
        partner = jnp.where(first_half,
                            pltpu.roll(t, DA_WIDTH - DA_HALF_DIM // 2, 1),
                            pltpu.roll(t, DA_HALF_DIM // 2, 1))
        return t * cos + partner * sin

    q_ref[...] = rope(proj(0, DA_WIDTH), cq_ref[...], sq_ref[...]).astype(q_ref.dtype)
    k_ref[...] = rope(proj(DA_WIDTH, 2 * DA_WIDTH), ck_ref[...], sk_ref[...]).astype(k_ref.dtype)
    v_ref[...] = proj(2 * DA_WIDTH, 3 * DA_WIDTH).astype(v_ref.dtype)
    rw_ref[...] = proj(DA_IN_WIDTH, DA_IN_WIDTH + RW_IN_WIDTH)
    gate_ref[...] = proj(DA_IN_WIDTH + RW_IN_WIDTH, D_IN)


def _inproj(x2d, g, w_bf, cq, sq, ck, sk, seq):
    m = x2d.shape[0]
    tm = INPROJ_ROWS
    nseq = seq // tm
    row = lambda i: (i, 0)
    tab = lambda i: (i % nseq, 0)
    return pl.pallas_call(
        _inproj_kernel,
        grid=(m // tm,),
        in_specs=[
            pl.BlockSpec((tm, D_MODEL), row),
            _resident((1, D_MODEL)),
            _resident((D_MODEL, D_IN)),
            pl.BlockSpec((tm, DA_WIDTH), tab),
            pl.BlockSpec((tm, DA_WIDTH), tab),
            pl.BlockSpec((tm, DA_WIDTH), tab),
            pl.BlockSpec((tm, DA_WIDTH), tab),
        ],
        out_specs=[
            pl.BlockSpec((tm, DA_WIDTH), row),
            pl.BlockSpec((tm, DA_WIDTH), row),
            pl.BlockSpec((tm, DA_WIDTH), row),
            pl.BlockSpec((tm, RW_IN_WIDTH), row),
            pl.BlockSpec((tm, GATE_WIDTH), row),
        ],
        out_shape=[
            jax.ShapeDtypeStruct((m, DA_WIDTH), BF16),
            jax.ShapeDtypeStruct((m, DA_WIDTH), BF16),
            jax.ShapeDtypeStruct((m, DA_WIDTH), BF16),
            jax.ShapeDtypeStruct((m, RW_IN_WIDTH), F32),
            jax.ShapeDtypeStruct((m, GATE_WIDTH), F32),
        ],
        compiler_params=pltpu.CompilerParams(
            dimension_semantics=("parallel",), vmem_limit_bytes=VMEM_LIMIT_BYTES),
        name="inproj",
    )(x2d, g, w_bf, cq, sq, ck, sk)


def _attn_kernel(lq1_ref, lk1_ref, lq2_ref, lk2_ref, sg_ref, q_ref, k_ref, v_ref,
                 o_ref):
    tq = q_ref.shape[0]
    i = pl.program_id(2)
    q = q_ref[...]
    lane = lax.broadcasted_iota(jnp.int32, q.shape, 1)
    zero = jnp.zeros_like(q)
    qq = jnp.concatenate([jnp.where(lane < DA_HALF_DIM, q, zero),
                          jnp.where(lane >= DA_HALF_DIM, q, zero)], axis=0)

    lam = (jnp.exp(jnp.sum(lq1_ref[...] * lk1_ref[...], axis=-1, keepdims=True))
           - jnp.exp(jnp.sum(lq2_ref[...] * lk2_ref[...], axis=-1, keepdims=True))
           + LAMBDA_INIT)
    r = lax.broadcasted_iota(jnp.int32, (2 * tq, tq), 0)
    c = lax.broadcasted_iota(jnp.int32, (2 * tq, tq), 1)
    allowed = (c // CHUNK) <= ((r & (tq - 1)) // CHUNK)

    def attend(nfull):
        d0 = nfull * tq
        s_d = jnp.where(allowed, _dot_nt(qq, k_ref[d0:d0 + tq, :]), -jnp.inf)
        m = jnp.max(s_d, axis=-1, keepdims=True)
        if nfull:
            s_f = _dot_nt(qq, k_ref[0:d0, :])
            m = jnp.maximum(m, jnp.max(s_f, axis=-1, keepdims=True))
        p_d = jnp.exp(s_d - m)
        l = jnp.sum(p_d, axis=-1, keepdims=True)
        acc = _dot(p_d.astype(BF16), v_ref[d0:d0 + tq, :])
        if nfull:
            p_f = jnp.exp(s_f - m)
            l = l + jnp.sum(p_f, axis=-1, keepdims=True)
            acc = acc + _dot(p_f.astype(BF16), v_ref[0:d0, :])
        o = acc / l
        od = o[:tq] - lam * o[tq:]
        ms = jnp.mean(od * od, axis=-1, keepdims=True)
        y = od * lax.rsqrt(ms + DA_SUBLN_EPS) * sg_ref[...]
        o_ref[...] = (y * (1.0 - LAMBDA_INIT)).astype(o_ref.dtype)

    for n in range(k_ref.shape[0] // tq):
        pl.when(i == n)(functools.partial(attend, n))


def _attention(q, k, v, lq1, lk1, lq2, lk2, subln_g):
    b, s, _ = q.shape
    tq = ATTN_BLOCK
    assert tq & (tq - 1) == 0 and s % tq == 0
    vec = lambda n: pl.BlockSpec((1, n), lambda bi, hi, qi: (0, 0))
    return pl.pallas_call(
        _attn_kernel,
        grid=(b, DA_HEADS, s // tq),
        in_specs=[
            vec(DA_HALF_DIM), vec(DA_HALF_DIM), vec(DA_HALF_DIM), vec(DA_HALF_DIM),
            vec(DA_VDIM),
            pl.BlockSpec((None, tq, DA_VDIM), lambda bi, hi, qi: (bi, qi, hi)),
            pl.BlockSpec((None, s, DA_VDIM), lambda bi, hi, qi: (bi, 0, hi)),
            pl.BlockSpec((None, s, DA_VDIM), lambda bi, hi, qi: (bi, 0, hi)),
        ],
        out_specs=pl.BlockSpec((None, tq, DA_VDIM), lambda bi, hi, qi: (bi, qi, hi)),
        out_shape=jax.ShapeDtypeStruct((b, s, DA_WIDTH), BF16),
        compiler_params=pltpu.CompilerParams(
            dimension_semantics=("parallel", "parallel", "arbitrary"),
            vmem_limit_bytes=VMEM_LIMIT_BYTES),
        name="diff_attn",
    )(lq1, lk1, lq2, lk2, subln_g, q, k, v)


def _rwkv_prep_kernel(rw_ref, prev_ref, mu_ref, w0_ref, a0_ref, kk_ref, ka_ref, rk_ref,
                      wa2_ref, g2_ref, ones_ref, tri_ref,
                      rm_ref, yg_ref, bonus_ref, g_ref):
    i = pl.program_id(1)
    tt = rw_ref.shape[0]
    z = rw_ref[...]
    prev = jnp.where(i > 0, prev_ref[7:8, :], 0.0)
    row = lax.broadcasted_iota(jnp.int32, z.shape, 0)
    z_prev = jnp.where(row == 0, prev, pltpu.roll(z, 1, 0))
    zs = z + (z_prev - z) * mu_ref[...]

    w3 = 3 * RW_WIDTH
    r = zs[:, 0:RW_WIDTH]
    k = zs[:, RW_WIDTH:2 * RW_WIDTH]
    v = zs[:, 2 * RW_WIDTH:w3]
    x_wa = zs[:, w3:w3 + LANES]
    x_g = zs[:, w3 + LANES:w3 + 2 * LANES]
    lane = lax.broadcasted_iota(jnp.int32, x_wa.shape, 1)
    t_wa = jnp.where(lane < RW_DECAY_LORA, jnp.tanh(x_wa), x_wa)
    lora = _dot(t_wa.astype(BF16), wa2_ref[...])
    u = -(w0_ref[...] + lora[:, :RW_WIDTH])
    softplus = jnp.maximum(u, 0.0) + jnp.log1p(jnp.exp(-jnp.abs(u)))
    log_decay = -jnp.exp(-softplus - 0.5)
    a_lr = jax.nn.sigmoid(a0_ref[...] + lora[:, RW_WIDTH:])
    g_ref[...] = _dot(jax.nn.sigmoid(x_g).astype(BF16), g2_ref[...])

    ones_bd = ones_ref[...]
    kk = k * kk_ref[...]
    kk = kk / jnp.maximum(jnp.sqrt(_segsum(kk * kk, ones_bd)), 1e-12)
    k2 = k * (1.0 + (a_lr - 1.0) * ka_ref[...])
    a_vec = -kk
    b_vec = kk * a_lr
    bonus_ref[...] = _segsum(r * k2 * rk_ref[...], ones_bd) * v

    tri = tri_ref[...]
    h1 = log_decay.astype(BF16)
    r1 = log_decay - h1.astype(F32)
    h2 = r1.astype(BF16)
    h3 = (r1 - h2.astype(F32)).astype(BF16)
    cs = _dot(tri, h1) + _dot(tri, h2) + _dot(tri, h3)
    nchunk = tt // CHUNK
    cl = jnp.concatenate(
        [jnp.broadcast_to(cs[(c + 1) * CHUNK - 1:(c + 1) * CHUNK, :], (CHUNK, RW_WIDTH))
         for c in range(nchunk)], axis=0)
    gam_inv = jnp.exp(-cs)
    gam_end = jnp.exp(cl - cs)
    r_t = r * jnp.exp(cs)
    a_t = a_vec * jnp.exp(cs - log_decay)
    k_t = k2 * gam_inv
    b_t = b_vec * gam_inv
    k_h = k2 * gam_end
    b_h = b_vec * gam_end
    gam_c = jnp.exp(cl)

    quad = 2 * LANES
    cq = (CHUNK, quad)
    t_idx = lax.broadcasted_iota(jnp.int32, cq, 0)
    l_idx = lax.broadcasted_iota(jnp.int32, cq, 1)
    s_idx = l_idx & (RW_HEAD - 1)
    strict = s_idx < t_idx
    incl = s_idx <= t_idx
    diag = s_idx == t_idx
    even_head = (l_idx & RW_HEAD) == 0
    r4 = lax.broadcasted_iota(jnp.int32, (quad, quad), 0)
    l4 = lax.broadcasted_iota(jnp.int32, (quad, quad), 1)
    bd4_mask = (r4 // RW_HEAD) == (l4 // RW_HEAD)
    r2 = lax.broadcasted_iota(jnp.int32, (2 * CHUNK, LANES), 0)
    l2 = lax.broadcasted_iota(jnp.int32, (2 * CHUNK, LANES), 1)
    bd_mask = (r2 < CHUNK) == (l2 < RW_HEAD)
    eye = (r2 == l2).astype(F32)
    eye2 = jnp.concatenate([eye, eye], axis=0)
    z_pair = jnp.zeros((2 * CHUNK, LANES), F32)
    z_half = jnp.zeros((CHUNK, LANES), F32)

    def bd(x):
        return jnp.where(bd_mask, jnp.concatenate([x, x], axis=0), 0.0)

    def bd4(x):
        return jnp.where(bd4_mask, jnp.concatenate([x, x, x, x], axis=0), 0.0)

    def blockdiag2(xa, xb):
        return jnp.concatenate([jnp.concatenate([xa, z_pair], axis=1),
                                jnp.concatenate([z_pair, xb], axis=1)], axis=0)

    def stacked_bd(st):
        return blockdiag2(st[:2 * CHUNK], st[2 * CHUNK:])

    items = [(c, qd) for c in range(nchunk) for qd in range(RW_WIDTH // quad)]

    def view(x, item):
        c, qd = item
        return x[c * CHUNK:(c + 1) * CHUNK, qd * quad:(qd + 1) * quad]

    a2 = [_dot_nt(jnp.concatenate([view(a_t, it), view(r_t, it)], axis=0).astype(BF16),
                  jnp.concatenate([bd4(view(b_t, it)), bd4(view(k_t, it))], axis=0).astype(BF16))
          for it in items]
    a_ab = [jnp.where(strict, z[:CHUNK, :quad], 0.0) for z in a2]
    a_ak = [jnp.where(strict, z[:CHUNK, quad:], 0.0) for z in a2]
    a_rb = [jnp.where(incl, z[CHUNK:, :quad], 0.0) for z in a2]
    a_rk = [jnp.where(incl, z[CHUNK:, quad:], 0.0) for z in a2]

    n_st = [jnp.concatenate([bd(z[:, :LANES]), bd(z[:, LANES:])], axis=0) for z in a_ab]
    t_st = [eye2 + n for n in n_st]
    p_st = [_dot(stacked_bd(n).astype(BF16), n.astype(BF16)) for n in n_st]
    for _ in range(4):
        zz = [_dot(stacked_bd(p).astype(BF16), jnp.concatenate([p, t], axis=1).astype(BF16))
              for p, t in zip(p_st, t_st)]
        p_st = [z[:, :LANES] for z in zz]
        t_st = [t + z[:, LANES:] for t, z in zip(t_st, zz)]
    t_st = [t + _dot(stacked_bd(p).astype(BF16), t.astype(BF16)) for p, t in zip(p_st, t_st)]

    bd4_v = [bd4(view(v, it)) for it in items]
    av = [_dot(x.astype(BF16), bv.astype(BF16)) for x, bv in zip(a_ak, bd4_v)]
    x_st = [jnp.concatenate(
        [jnp.concatenate([bd(view(a_t, it)[:, :LANES]), bd(u[:, :LANES])], axis=1),
         jnp.concatenate([bd(view(a_t, it)[:, LANES:]), bd(u[:, LANES:])], axis=1)], axis=0)
        for it, u in zip(items, av)]
    tx = [_dot(stacked_bd(t).astype(BF16), x.astype(BF16)) for t, x in zip(t_st, x_st)]

    for it, tx_i, arb, ark, bv4 in zip(items, tx, a_rb, a_rk, bd4_v):
        c, qd = it
        r_bar, y_bar, au = [], [], []
        for half in range(2):
            pl_ = slice(half * LANES, (half + 1) * LANES)
            rows = slice(half * 2 * CHUNK, (half + 1) * 2 * CHUNK)
            tx_p = tx_i[rows]
            rhs = jnp.concatenate(
                [tx_p, jnp.concatenate([z_pair, bv4[rows, pl_]], axis=1)], axis=0)
            ry = _dot(jnp.concatenate([arb[:, pl_], ark[:, pl_]], axis=1).astype(BF16),
                      rhs.astype(BF16))
            r_bar.append(view(r_t, it)[:, pl_] + ry[:, :LANES])
            y_bar.append(ry[:, LANES:])
            au.append(tx_p[:CHUNK] + tx_p[CHUNK:])
        v_q = view(v, it)
        mg_rhs = jnp.concatenate(
            [au[0], jnp.concatenate([z_half, v_q[:, :LANES]], axis=1),
             au[1], jnp.concatenate([z_half, v_q[:, LANES:]], axis=1)], axis=0)
        lhs_t = jnp.concatenate([view(b_h, it), view(k_h, it)], axis=0).T
        mg = _dot(stacked_bd(lhs_t).astype(BF16), mg_rhs.astype(BF16))
        m_sel = jnp.concatenate([mg[0:CHUNK], mg[2 * CHUNK:3 * CHUNK]], axis=1)
        m_odd = jnp.concatenate([mg[CHUNK:2 * CHUNK], mg[3 * CHUNK:]], axis=1)
        m_even_odd = jnp.where(jnp.concatenate([even_head, even_head], axis=1), m_sel, m_odd)
        m_quad = (jnp.concatenate([m_even_odd[:, :LANES], m_even_odd[:, 2 * LANES:3 * LANES]], axis=1)
                  + jnp.where(diag, view(gam_c, it), 0.0))
        g_quad = jnp.concatenate([m_even_odd[:, LANES:2 * LANES], m_even_odd[:, 3 * LANES:]], axis=1)
        cols = slice(qd * quad, (qd + 1) * quad)
        rm_ref[c, :, cols] = jnp.concatenate(
            [jnp.concatenate(r_bar, axis=1), m_quad], axis=0).astype(rm_ref.dtype)
        yg_ref[c, :, cols] = jnp.concatenate([jnp.concatenate(y_bar, axis=1), g_quad], axis=0)


def _rwkv_prep(rw, mu, w0, a0, k_k, k_a, r_k, wa2, g2, ones_bd, tri):
    b, s, _ = rw.shape
    tt = RW_PREP_ROWS
    cpt = tt // CHUNK
    nc = s // CHUNK
    vec = lambda n: pl.BlockSpec((1, n), lambda bi, ti: (0, 0))
    return pl.pallas_call(
        _rwkv_prep_kernel,
        grid=(b, s // tt),
        in_specs=[
            pl.BlockSpec((None, tt, RW_IN_WIDTH), lambda bi, ti: (bi, ti, 0)),
            pl.BlockSpec((None, 8, RW_IN_WIDTH),
                         lambda bi, ti: (bi, jnp.maximum(ti * (tt // 8) - 1, 0), 0)),
            vec(RW_IN_WIDTH), vec(RW_WIDTH), vec(RW_WIDTH), vec(RW_WIDTH), vec(RW_WIDTH),
            vec(RW_WIDTH),
            pl.BlockSpec((LANES, 2 * RW_WIDTH), lambda bi, ti: (0, 0)),
            pl.BlockSpec((RW_GATE_LORA, RW_WIDTH), lambda bi, ti: (0, 0)),
            pl.BlockSpec((RW_WIDTH, RW_WIDTH), lambda bi, ti: (0, 0)),
            pl.BlockSpec((tt, tt), lambda bi, ti: (0, 0)),
        ],
        out_specs=[
            pl.BlockSpec((None, cpt, 2 * CHUNK, RW_WIDTH), lambda bi, ti: (bi, ti, 0, 0)),
            pl.BlockSpec((None, cpt, 2 * CHUNK, RW_WIDTH), lambda bi, ti: (bi, ti, 0, 0)),
            pl.BlockSpec((None, tt, RW_WIDTH), lambda bi, ti: (bi, ti, 0)),
            pl.BlockSpec((None, tt, RW_WIDTH), lambda bi, ti: (bi, ti, 0)),
        ],
        out_shape=[
            jax.ShapeDtypeStruct((b, nc, 2 * CHUNK, RW_WIDTH), BF16),
            jax.ShapeDtypeStruct((b, nc, 2 * CHUNK, RW_WIDTH), F32),
            jax.ShapeDtypeStruct((b, s, RW_WIDTH), F32),
            jax.ShapeDtypeStruct((b, s, RW_WIDTH), F32),
        ],
        compiler_params=pltpu.CompilerParams(
            dimension_semantics=("parallel", "parallel"), vmem_limit_bytes=VMEM_LIMIT_BYTES),
        name="rwkv_prep",
    )(rw, rw, mu, w0, a0, k_k, k_a, r_k, wa2, g2, ones_bd, tri)


def _rwkv_scan_kernel(rm_ref, yg_ref, bonus_ref, g_ref, lng_ref, lnb_ref, ones_ref,
                      y_ref, h_sc, y_sc):
    c = pl.program_id(0)
    nb = rm_ref.shape[0]

    @pl.when(c == 0)
    def _():
        h_sc[...] = jnp.zeros(h_sc.shape, F32)

    r128 = lax.broadcasted_iota(jnp.int32, (2 * CHUNK, LANES), 0)
    l128 = lax.broadcasted_iota(jnp.int32, (2 * CHUNK, LANES), 1)
    bd_mask = (r128 < CHUNK) == (l128 < RW_HEAD)

    def body(bi, carry):
        for p in range(RW_PAIRS):
            lanes = slice(p * LANES, (p + 1) * LANES)
            h_bd = h_sc[bi * RW_PAIRS + p]
            out = _dot(rm_ref[bi, :, lanes], h_bd.astype(BF16)) + yg_ref[bi, :, lanes]
            y_sc[bi, :, lanes] = out[:CHUNK]
            h_new = out[CHUNK:]
            h_sc[bi * RW_PAIRS + p] = jnp.where(
                bd_mask, jnp.concatenate([h_new, h_new], axis=0), 0.0)
        return carry

    lax.fori_loop(0, nb, body, 0)

    ones_bd = ones_ref[...]
    y = y_sc[...].reshape(nb * CHUNK, RW_WIDTH)
    mean = _segsum(y, ones_bd) * (1.0 / RW_HEAD)
    d = y - mean
    var = _segsum(d * d, ones_bd) * (1.0 / RW_HEAD)
    yn = d * lax.rsqrt(var + RW_GN_EPS) * lng_ref[...] + lnb_ref[...]
    bonus = bonus_ref[...].reshape(nb * CHUNK, RW_WIDTH)
    gate = g_ref[...].reshape(nb * CHUNK, RW_WIDTH)
    y_ref[...] = ((yn + bonus) * gate).reshape(nb, CHUNK, RW_WIDTH).astype(y_ref.dtype)


def _rwkv_scan(rm, yg, bonus, g, ln_g, ln_b, ones_bd):
    b, nc, _, _ = rm.shape
    s = nc * CHUNK
    vec = lambda n: pl.BlockSpec((1, n), lambda ci: (0, 0))
    return pl.pallas_call(
        _rwkv_scan_kernel,
        grid=(nc,),
        in_specs=[
            pl.BlockSpec((b, None, 2 * CHUNK, RW_WIDTH), lambda ci: (0, ci, 0, 0)),
            pl.BlockSpec((b, None, 2 * CHUNK, RW_WIDTH), lambda ci: (0, ci, 0, 0)),
            pl.BlockSpec((b, CHUNK, RW_WIDTH), lambda ci: (0, ci, 0)),
            pl.BlockSpec((b, CHUNK, RW_WIDTH), lambda ci: (0, ci, 0)),
            vec(RW_WIDTH), vec(RW_WIDTH),
            pl.BlockSpec((RW_WIDTH, RW_WIDTH), lambda ci: (0, 0)),
        ],
        out_specs=pl.BlockSpec((b, CHUNK, RW_WIDTH), lambda ci: (0, ci, 0)),
        out_shape=jax.ShapeDtypeStruct((b, s, RW_WIDTH), BF16),
        scratch_shapes=[
            pltpu.VMEM((b * RW_PAIRS, 2 * CHUNK, LANES), F32),
            pltpu.VMEM((b, CHUNK, RW_WIDTH), F32),
        ],
        compiler_params=pltpu.CompilerParams(
            dimension_semantics=("arbitrary",), vmem_limit_bytes=VMEM_LIMIT_BYTES),
        name="rwkv_scan",
    )(rm, yg, bonus, g, ln_g, ln_b, ones_bd)


def _merge_ffn_kernel(x_ref, ya_ref, yb_ref, gate_ref, wa_ref, wb_ref, wo_ref,
                      gffn_ref, w1_ref, w2_ref, gfin_ref, o_ref):
    pa = _dot(ya_ref[...], wa_ref[...])
    pb = _dot(yb_ref[...], wb_ref[...])
    gate = gate_ref[...]
    merged = (jax.nn.sigmoid(gate[:, :D_MODEL]) * pa
              + jax.nn.sigmoid(gate[:, D_MODEL:]) * pb)
    x1 = x_ref[...] + _dot(merged.astype(BF16), wo_ref[...])
    ms = jnp.mean(x1 * x1, axis=-1, keepdims=True)
    h = (x1 * lax.rsqrt(ms + NORM_EPS) * gffn_ref[...]).astype(BF16)
    acc = x1
    for c in range(D_FF // FFN_CHUNK):
        cols = slice(c * FFN_CHUNK, (c + 1) * FFN_CHUNK)
        f = jnp.maximum(_dot(h, w1_ref[:, cols]), 0.0)
        acc = acc + _dot((f * f).astype(BF16), w2_ref[cols, :])
    ms2 = jnp.mean(acc * acc, axis=-1, keepdims=True)
    o_ref[...] = acc * lax.rsqrt(ms2 + NORM_EPS) * gfin_ref[...]


def _merge_ffn(x2d, ya, yb, gates, wa, wb, wo, gffn, w1, w2, gfin):
    m = x2d.shape[0]
    tm = FFN_ROWS
    row = lambda i: (i, 0)
    return pl.pallas_call(
        _merge_ffn_kernel,
        grid=(m // tm,),
        in_specs=[
            pl.BlockSpec((tm, D_MODEL), row),
            pl.BlockSpec((tm, DA_WIDTH), row),
            pl.BlockSpec((tm, RW_WIDTH), row),
            pl.BlockSpec((tm, GATE_WIDTH), row),
            _resident((DA_WIDTH, D_MODEL)),
            _resident((RW_WIDTH, D_MODEL)),
            _resident((D_MODEL, D_MODEL)),
            _resident((1, D_MODEL)),
            _resident((D_MODEL, D_FF)),
            _resident((D_FF, D_MODEL)),
            _resident((1, D_MODEL)),
        ],
        out_specs=pl.BlockSpec((tm, D_MODEL), row),
        out_shape=jax.ShapeDtypeStruct((m, D_MODEL), F32),
        compiler_params=pltpu.CompilerParams(
            dimension_semantics=("parallel",), vmem_limit_bytes=VMEM_LIMIT_BYTES),
        name="merge_ffn",
    )(x2d, ya, yb, gates, wa, wb, wo, gffn, w1, w2, gfin)


def _rope_tables(seq):
    d = DA_HALF_DIM
    pos = jnp.arange(seq, dtype=F32)
    inv_freq = ROPE_THETA ** (-jnp.arange(0, d, 2, dtype=F32) / d)
    ang = pos[:, None] * inv_freq[None, :]
    cos = jnp.cos(ang)
    sin = jnp.sin(ang)
    reps = DA_WIDTH // d
    cos_full = jnp.tile(jnp.concatenate([cos, cos], axis=-1), (1, reps))
    sin_full = jnp.tile(jnp.concatenate([-sin, sin], axis=-1), (1, reps))
    return cos_full, sin_full


def kernel(x, norm_mix_g, w_in, rw_mu, rw_w0, rw_w2, rw_a0, rw_a2, rw_g2, rw_k_k, rw_k_a,
           rw_r_k, rw_ln_g, rw_ln_b, da_lq1, da_lk1, da_lq2, da_lk2, da_subln_g,
           w_branch_a, w_branch_b, w_o, norm_ffn_g, w_ff1, w_ff2, norm_final_g):
    b, s, d = x.shape
    assert d == D_MODEL and norm_mix_g.shape[0] == 1
    assert s % ATTN_BLOCK == 0 and s % INPROJ_ROWS == 0 and s % RW_PREP_ROWS == 0
    assert (b * s) % FFN_ROWS == 0
    row = lambda t: t.reshape(1, -1)
    x2d = x.reshape(b * s, d)

    cos, sin = _rope_tables(s)
    scale = DA_HALF_DIM ** -0.5
    q, k, v, rw, gates = _inproj(x2d, norm_mix_g, w_in[0].astype(BF16),
                                 cos * scale, sin * scale, cos, sin, s)

    ya = _attention(q.reshape(b, s, DA_WIDTH), k.reshape(b, s, DA_WIDTH),
                    v.reshape(b, s, DA_WIDTH), row(da_lq1), row(da_lk1), row(da_lq2),
                    row(da_lk2), row(da_subln_g))

    wa2 = jnp.zeros((LANES, 2 * RW_WIDTH), F32)
    wa2 = wa2.at[:RW_DECAY_LORA, :RW_WIDTH].set(rw_w2[0])
    wa2 = wa2.at[RW_DECAY_LORA:, RW_WIDTH:].set(rw_a2[0])
    idx = jnp.arange(RW_WIDTH) // RW_HEAD
    ones_bd = (idx[:, None] == idx[None, :]).astype(BF16)
    tok = jnp.arange(RW_PREP_ROWS)
    tri = ((tok[:, None] >= tok[None, :])
           & (tok[:, None] // CHUNK == tok[None, :] // CHUNK)).astype(BF16)

    rm, yg, bonus, g = _rwkv_prep(
        rw.reshape(b, s, RW_IN_WIDTH), rw_mu, rw_w0, rw_a0, rw_k_k, rw_k_a, row(rw_r_k),
        wa2.astype(BF16), rw_g2[0].astype(BF16), ones_bd, tri)
    yb = _rwkv_scan(rm, yg, bonus, g, rw_ln_g, rw_ln_b, ones_bd)

    out = _merge_ffn(x2d, ya.reshape(b * s, DA_WIDTH), yb.reshape(b * s, RW_WIDTH), gates,
                     w_branch_a[0].astype(BF16), w_branch_b[0].astype(BF16),
                     w_o[0].astype(BF16), norm_ffn_g, w_ff1[0].astype(BF16),
                     w_ff2[0].astype(BF16), row(norm_final_g))
    return out.reshape(b, s, d)
```

```python
import functools
import math

import jax
import jax.numpy as jnp
from jax import lax
from jax.experimental import pallas as pl
from jax.experimental.pallas import tpu as pltpu

F32 = jnp.float32
BF16 = jnp.bfloat16

D_MODEL = 1024
CHUNK = 64
NORM_EPS = 1e-6
ROPE_THETA = 10000.0
DA_WIDTH = 512
DA_HEADS = 4
DA_HALF_DIM = 64
DA_VDIM = 128
DA_SUBLN_EPS = 1e-5
RW_WIDTH = 512
RW_HEAD = 64
RW_PAIRS = RW_WIDTH // (2 * RW_HEAD)
RW_DECAY_LORA = 64
RW_AAA_LORA = 64
RW_GATE_LORA = 128
RW_GN_EPS = RW_HEAD * 1e-5
RW_IN_WIDTH = 3 * RW_WIDTH + RW_DECAY_LORA + RW_AAA_LORA + RW_GATE_LORA
DA_IN_WIDTH = 3 * DA_WIDTH
GATE_WIDTH = 2 * D_MODEL
D_IN = DA_IN_WIDTH + RW_IN_WIDTH + GATE_WIDTH
D_FF = 4 * D_MODEL
LAMBDA_INIT = 0.8 - 0.6 * math.exp(0.0)

LANES = 128
VMEM_LIMIT_BYTES = 56 * 1024 * 1024

INPROJ_ROWS = 256
ATTN_BLOCK = 256
ATTN_HEAD_GROUP = 4
RW_PREP_ROWS = 256
FFN_ROWS = 256
FFN_CHUNK = 1024


def _dot(a, b):
    return jnp.dot(a, b, preferred_element_type=F32)


def _dot_nt(a, b):
    return lax.dot_general(a, b, (((1,), (1,)), ((), ())), preferred_element_type=F32)


def _split2(x):
    hi = x.astype(BF16)
    lo = (x - hi.astype(F32)).astype(BF16)
    return hi, lo


def _segsum(x, ones_bd):
    hi, lo = _split2(x)
    return _dot(hi, ones_bd) + _dot(lo, ones_bd)


def _resident(shape):
    nd = len(shape)
    return pl.BlockSpec(shape, lambda *_: (0,) * nd, pipeline_mode=pl.Buffered(1))


def _inproj_kernel(x_ref, g_ref, w_ref, cq_ref, sq_ref, ck_ref, sk_ref,
                   q_ref, k_ref, v_ref, rw_ref, gate_ref):
    x = x_ref[...]
    ms = jnp.mean(x * x, axis=-1, keepdims=True)
    h = (x * lax.rsqrt(ms + NORM_EPS) * g_ref[...]).astype(BF16)

    def proj(c0, c1):
        return _dot(h, w_ref[:, c0:c1])

    rows = x.shape[0]
    lane = lax.broadcasted_iota(jnp.int32, (rows, DA_WIDTH), 1)
    first_half = (lane & (DA_HALF_DIM - 1)) < (DA_HALF_DIM // 2)

    def rope(t, cos, sin):
        partner = jnp.where(first_half,
                            pltpu.roll(t, DA_WIDTH - DA_HALF_DIM // 2, 1),
                            pltpu.roll(t, DA_HALF_DIM // 2, 1))
        return t * cos + partner * sin

    q_ref[...] = rope(proj(0, DA_WIDTH), cq_ref[...], sq_ref[...]).astype(q_ref.dtype)
    k_ref[...] = rope(proj(DA_WIDTH, 2 * DA_WIDTH), ck_ref[...], sk_ref[...]).astype(k_ref.dtype)
    v_ref[...] = proj(2 * DA_WIDTH, 3 * DA_WIDTH).astype(v_ref.dtype)
    rw_ref[...] = proj(DA_IN_WIDTH, DA_IN_WIDTH + RW_IN_WIDTH)
    gate_ref[...] = proj(DA_IN_WIDTH + RW_IN_WIDTH, D_IN)


def _inproj(x2d, g, w_bf, cq, sq, ck, sk, seq):
    m = x2d.shape[0]
    tm = INPROJ_ROWS
    nseq = seq // tm
    row = lambda i: (i, 0)
    tab = lambda i: (i % nseq, 0)
    return pl.pallas_call(
        _inproj_kernel,
        grid=(m // tm,),
        in_specs=[
            pl.BlockSpec((tm, D_MODEL), row),
            _resident((1, D_MODEL)),
            _resident((D_MODEL, D_IN)),
            pl.BlockSpec((tm, DA_WIDTH), tab),
            pl.BlockSpec((tm, DA_WIDTH), tab),
            pl.BlockSpec((tm, DA_WIDTH), tab),
            pl.BlockSpec((tm, DA_WIDTH), tab),
        ],
        out_specs=[
            pl.BlockSpec((tm, DA_WIDTH), row),
            pl.BlockSpec((tm, DA_WIDTH), row),
            pl.BlockSpec((tm, DA_WIDTH), row),
            pl.BlockSpec((tm, RW_IN_WIDTH), row),
            pl.BlockSpec((tm, GATE_WIDTH), row),
        ],
        out_shape=[
            jax.ShapeDtypeStruct((m, DA_WIDTH), BF16),
            jax.ShapeDtypeStruct((m, DA_WIDTH), BF16),
            jax.ShapeDtypeStruct((m, DA_WIDTH), BF16),
            jax.ShapeDtypeStruct((m, RW_IN_WIDTH), F32),
            jax.ShapeDtypeStruct((m, GATE_WIDTH), F32),
        ],
        compiler_params=pltpu.CompilerParams(
            dimension_semantics=("parallel",), vmem_limit_bytes=VMEM_LIMIT_BYTES),
        name="inproj",
    )(x2d, g, w_bf, cq, sq, ck, sk)


def _attn_kernel(lq1_ref, lk1_ref, lq2_ref, lk2_ref, sg_ref, q_ref, k_ref, v_ref,
                 o_ref):
    tq = q_ref.shape[0]
    heads = q_ref.shape[1] // DA_VDIM
    i = pl.program_id(2)
    lane = lax.broadcasted_iota(jnp.int32, (tq, DA_VDIM), 1)
    lam = (jnp.exp(jnp.sum(lq1_ref[...] * lk1_ref[...], axis=-1, keepdims=True))
           - jnp.exp(jnp.sum(lq2_ref[...] * lk2_ref[...], axis=-1, keepdims=True))
           + LAMBDA_INIT)
    r = lax.broadcasted_iota(jnp.int32, (2 * tq, tq), 0)
    c = lax.broadcasted_iota(jnp.int32, (2 * tq, tq), 1)
    allowed = (c // CHUNK) <= ((r & (tq - 1)) // CHUNK)

    def attend(nfull, h):
        cols = slice(h * DA_VDIM, (h + 1) * DA_VDIM)
        q = q_ref[:, cols]
        zero = jnp.zeros_like(q)
        qq = jnp.concatenate([jnp.where(lane < DA_HALF_DIM, q, zero),
                              jnp.where(lane >= DA_HALF_DIM, q, zero)], axis=0)
        s = []
        for j in range(nfull + 1):
            s_j = _dot_nt(qq, k_ref[j * tq:(j + 1) * tq, cols])
            s.append(jnp.where(allowed, s_j, -jnp.inf) if j == nfull else s_j)
        m_lane = s[0]
        for s_j in s[1:]:
            m_lane = jnp.maximum(m_lane, s_j)
        m = jnp.max(m_lane, axis=-1, keepdims=True)
        l_lane = None
        acc = None
        for j, s_j in enumerate(s):
            p = jnp.exp(s_j - m)
            pv = _dot(p.astype(BF16), v_ref[j * tq:(j + 1) * tq, cols])
            l_lane = p if l_lane is None else l_lane + p
            acc = pv if acc is None else acc + pv
        l = jnp.sum(l_lane, axis=-1, keepdims=True)
        o = acc / l
        od = o[:tq] - lam * o[tq:]
        ms = jnp.mean(od * od, axis=-1, keepdims=True)
        y = od * lax.rsqrt(ms + DA_SUBLN_EPS) * sg_ref[...]
        o_ref[:, cols] = (y * (1.0 - LAMBDA_INIT)).astype(o_ref.dtype)

    def variant(nfull):
        for h in range(heads):
            attend(nfull, h)

    for n in range(k_ref.shape[0] // tq):
        pl.when(i == n)(functools.partial(variant, n))


def _attention(q, k, v, lq1, lk1, lq2, lk2, subln_g):
    b, s, _ = q.shape
    tq = ATTN_BLOCK
    assert tq & (tq - 1) == 0 and s % tq == 0
    vec = lambda n: pl.BlockSpec((1, n), lambda bi, hi, qi: (0, 0))
    gw = ATTN_HEAD_GROUP * DA_VDIM
    return pl.pallas_call(
        _attn_kernel,
        grid=(b, DA_HEADS // ATTN_HEAD_GROUP, s // tq),
        in_specs=[
            vec(DA_HALF_DIM), vec(DA_HALF_DIM), vec(DA_HALF_DIM), vec(DA_HALF_DIM),
            vec(DA_VDIM),
            pl.BlockSpec((None, tq, gw), lambda bi, hi, qi: (bi, qi, hi)),
            pl.BlockSpec((None, s, gw), lambda bi, hi, qi: (bi, 0, hi)),
            pl.BlockSpec((None, s, gw), lambda bi, hi, qi: (bi, 0, hi)),
        ],
        out_specs=pl.BlockSpec((None, tq, gw), lambda bi, hi, qi: (bi, qi, hi)),
        out_shape=jax.ShapeDtypeStruct((b, s, DA_WIDTH), BF16),
        compiler_params=pltpu.CompilerParams(
            dimension_semantics=("parallel", "parallel", "arbitrary"),
            vmem_limit_bytes=VMEM_LIMIT_BYTES),
        name="diff_attn",
    )(lq1, lk1, lq2, lk2, subln_g, q, k, v)


def _rwkv_prep_kernel(rw_ref, prev_ref, mu_ref, w0_ref, a0_ref, kk_ref, ka_ref, rk_ref,
                      wa2_ref, g2_ref, ones_ref, tri_ref,
                      rm_ref, yg_ref, bonus_ref, g_ref):
    i = pl.program_id(1)
    tt = rw_ref.shape[0]
    z = rw_ref[...]
    prev = jnp.where(i > 0, prev_ref[7:8, :], 0.0)
    row = lax.broadcasted_iota(jnp.int32, z.shape, 0)
    z_prev = jnp.where(row == 0, prev, pltpu.roll(z, 1, 0))
    zs = z + (z_prev - z) * mu_ref[...]

    w3 = 3 * RW_WIDTH
    r = zs[:, 0:RW_WIDTH]
    k = zs[:, RW_WIDTH:2 * RW_WIDTH]
    v = zs[:, 2 * RW_WIDTH:w3]
    x_wa = zs[:, w3:w3 + LANES]
    x_g = zs[:, w3 + LANES:w3 + 2 * LANES]
    lane = lax.broadcasted_iota(jnp.int32, x_wa.shape, 1)
    t_wa = jnp.where(lane < RW_DECAY_LORA, jnp.tanh(x_wa), x_wa)
    lora = _dot(t_wa.astype(BF16), wa2_ref[...])
    u = -(w0_ref[...] + lora[:, :RW_WIDTH])
    softplus = jnp.maximum(u, 0.0) + jnp.log1p(jnp.exp(-jnp.abs(u)))
    log_decay = -jnp.exp(-softplus - 0.5)
    a_lr = jax.nn.sigmoid(a0_ref[...] + lora[:, RW_WIDTH:])
    g_ref[...] = _dot(jax.nn.sigmoid(x_g).astype(BF16), g2_ref[...])

    ones_bd = ones_ref[...]
    kk = k * kk_ref[...]
    kk = kk / jnp.maximum(jnp.sqrt(_segsum(kk * kk, ones_bd)), 1e-12)
    k2 = k * (1.0 + (a_lr - 1.0) * ka_ref[...])
    a_vec = -kk
    b_vec = kk * a_lr
    bonus_ref[...] = _segsum(r * k2 * rk_ref[...], ones_bd) * v

    tri = tri_ref[...]
    h1 = log_decay.astype(BF16)
    r1 = log_decay - h1.astype(F32)
    h2 = r1.astype(BF16)
    h3 = (r1 - h2.astype(F32)).astype(BF16)
    cs = _dot(tri, h1) + _dot(tri, h2) + _dot(tri, h3)
    nchunk = tt // CHUNK
    cl = jnp.concatenate(
        [jnp.broadcast_to(cs[(c + 1) * CHUNK - 1:(c + 1) * CHUNK, :], (CHUNK, RW_WIDTH))
         for c in range(nchunk)], axis=0)
    gam_inv = jnp.exp(-cs)
    gam_end = jnp.exp(cl - cs)
    r_t = r * jnp.exp(cs)
    a_t = a_vec * jnp.exp(cs - log_decay)
    k_t = k2 * gam_inv
    b_t = b_vec * gam_inv
    k_h = k2 * gam_end
    b_h = b_vec * gam_end
    gam_c = jnp.exp(cl)

    quad = 2 * LANES
    cq = (CHUNK, quad)
    t_idx = lax.broadcasted_iota(jnp.int32, cq, 0)
    l_idx = lax.broadcasted_iota(jnp.int32, cq, 1)
    s_idx = l_idx & (RW_HEAD - 1)
    strict = s_idx < t_idx
    incl = s_idx <= t_idx
    diag = s_idx == t_idx
    even_head = (l_idx & RW_HEAD) == 0
    r4 = lax.broadcasted_iota(jnp.int32, (quad, quad), 0)
    l4 = lax.broadcasted_iota(jnp.int32, (quad, quad), 1)
    bd4_mask = (r4 // RW_HEAD) == (l4 // RW_HEAD)
    r2 = lax.broadcasted_iota(jnp.int32, (2 * CHUNK, LANES), 0)
    l2 = lax.broadcasted_iota(jnp.int32, (2 * CHUNK, LANES), 1)
    bd_mask = (r2 < CHUNK) == (l2 < RW_HEAD)
    eye = (r2 == l2).astype(F32)
    eye2 = jnp.concatenate([eye, eye], axis=0)
    z_pair = jnp.zeros((2 * CHUNK, LANES), F32)
    z_half = jnp.zeros((CHUNK, LANES), F32)

    def bd(x):
        return jnp.where(bd_mask, jnp.concatenate([x, x], axis=0), 0.0)

    def bd4(x):
        return jnp.where(bd4_mask, jnp.concatenate([x, x, x, x], axis=0), 0.0)

    def blockdiag2(xa, xb):
        return jnp.concatenate([jnp.concatenate([xa, z_pair], axis=1),
                                jnp.concatenate([z_pair, xb], axis=1)], axis=0)

    def stacked_bd(st):
        return blockdiag2(st[:2 * CHUNK], st[2 * CHUNK:])

    items = [(c, qd) for c in range(nchunk) for qd in range(RW_WIDTH // quad)]

    def view(x, item):
        c, qd = item
        return x[c * CHUNK:(c + 1) * CHUNK, qd * quad:(qd + 1) * quad]

    a2 = [_dot_nt(jnp.concatenate([view(a_t, it), view(r_t, it)], axis=0).astype(BF16),
                  jnp.concatenate([bd4(view(b_t, it)), bd4(view(k_t, it))], axis=0).astype(BF16))
          for it in items]
    a_ab = [jnp.where(strict, z[:CHUNK, :quad], 0.0) for z in a2]
    a_ak = [jnp.where(strict, z[:CHUNK, quad:], 0.0) for z in a2]
    a_rb = [jnp.where(incl, z[CHUNK:, :quad], 0.0) for z in a2]
    a_rk = [jnp.where(incl, z[CHUNK:, quad:], 0.0) for z in a2]

    n_st = [jnp.concatenate([bd(z[:, :LANES]), bd(z[:, LANES:])], axis=0) for z in a_ab]
    t_st = [eye2 + n for n in n_st]
    p_st = [_dot(stacked_bd(n).astype(BF16), n.astype(BF16)) for n in n_st]
    for _ in range(4):
        zz = [_dot(stacked_bd(p).astype(BF16), jnp.concatenate([p, t], axis=1).astype(BF16))
              for p, t in zip(p_st, t_st)]
        p_st = [z[:, :LANES] for z in zz]
        t_st = [t + z[:, LANES:] for t, z in zip(t_st, zz)]
    t_st = [t + _dot(stacked_bd(p).astype(BF16), t.astype(BF16)) for p, t in zip(p_st, t_st)]

    bd4_v = [bd4(view(v, it)) for it in items]
    av = [_dot(x.astype(BF16), bv.astype(BF16)) for x, bv in zip(a_ak, bd4_v)]
    x_st = [jnp.concatenate(
        [jnp.concatenate([bd(view(a_t, it)[:, :LANES]), bd(u[:, :LANES])], axis=1),
         jnp.concatenate([bd(view(a_t, it)[:, LANES:]), bd(u[:, LANES:])], axis=1)], axis=0)
        for it, u in zip(items, av)]
    tx = [_dot(stacked_bd(t).astype(BF16), x.astype(BF16)) for t, x in zip(t_st, x_st)]

    for it, tx_i, arb, ark, bv4 in zip(items, tx, a_rb, a_rk, bd4_v):
        c, qd = it
        r_bar, y_bar, au = [], [], []
        for half in range(2):
            pl_ = slice(half * LANES, (half + 1) * LANES)
            rows = slice(half * 2 * CHUNK, (half + 1) * 2 * CHUNK)
            tx_p = tx_i[rows]
            rhs = jnp.concatenate(
                [tx_p, jnp.concatenate([z_pair, bv4[rows, pl_]], axis=1)], axis=0)
            ry = _dot(jnp.concatenate([arb[:, pl_], ark[:, pl_]], axis=1).astype(BF16),
                      rhs.astype(BF16))
            r_bar.append(view(r_t, it)[:, pl_] + ry[:, :LANES])
            y_bar.append(ry[:, LANES:])
            au.append(tx_p[:CHUNK] + tx_p[CHUNK:])
        v_q = view(v, it)
        mg_rhs = jnp.concatenate(
            [au[0], jnp.concatenate([z_half, v_q[:, :LANES]], axis=1),
             au[1], jnp.concatenate([z_half, v_q[:, LANES:]], axis=1)], axis=0)
        lhs_t = jnp.concatenate([view(b_h, it), view(k_h, it)], axis=0).T
        mg = _dot(stacked_bd(lhs_t).astype(BF16), mg_rhs.astype(BF16))
        m_sel = jnp.concatenate([mg[0:CHUNK], mg[2 * CHUNK:3 * CHUNK]], axis=1)
        m_odd = jnp.concatenate([mg[CHUNK:2 * CHUNK], mg[3 * CHUNK:]], axis=1)
        m_even_odd = jnp.where(jnp.concatenate([even_head, even_head], axis=1), m_sel, m_odd)
        m_quad = (jnp.concatenate([m_even_odd[:, :LANES], m_even_odd[:, 2 * LANES:3 * LANES]], axis=1)
                  + jnp.where(diag, view(gam_c, it), 0.0))
        g_quad = jnp.concatenate([m_even_odd[:, LANES:2 * LANES], m_even_odd[:, 3 * LANES:]], axis=1)
        cols = slice(qd * quad, (qd + 1) * quad)
        rm_ref[c, :, cols] = jnp.concatenate(
            [jnp.concatenate(r_bar, axis=1), m_quad], axis=0).astype(rm_ref.dtype)
        yg_ref[c, :, cols] = jnp.concatenate([jnp.concatenate(y_bar, axis=1), g_quad], axis=0)


def _rwkv_prep(rw, mu, w0, a0, k_k, k_a, r_k, wa2, g2, ones_bd, tri):
    b, s, _ = rw.shape
    tt = RW_PREP_ROWS
    cpt = tt // CHUNK
    nc = s // CHUNK
    vec = lambda n: pl.BlockSpec((1, n), lambda bi, ti: (0, 0))
    return pl.pallas_call(
        _rwkv_prep_kernel,
        grid=(b, s // tt),
        in_specs=[
            pl.BlockSpec((None, tt, RW_IN_WIDTH), lambda bi, ti: (bi, ti, 0)),
            pl.BlockSpec((None, 8, RW_IN_WIDTH),
                         lambda bi, ti: (bi, jnp.maximum(ti * (tt // 8) - 1, 0), 0)),
            vec(RW_IN_WIDTH), vec(RW_WIDTH), vec(RW_WIDTH), vec(RW_WIDTH), vec(RW_WIDTH),
            vec(RW_WIDTH),
            pl.BlockSpec((LANES, 2 * RW_WIDTH), lambda bi, ti: (0, 0)),
            pl.BlockSpec((RW_GATE_LORA, RW_WIDTH), lambda bi, ti: (0, 0)),
            pl.BlockSpec((RW_WIDTH, RW_WIDTH), lambda bi, ti: (0, 0)),
            pl.BlockSpec((tt, tt), lambda bi, ti: (0, 0)),
        ],
        out_specs=[
            pl.BlockSpec((None, cpt, 2 * CHUNK, RW_WIDTH), lambda bi, ti: (bi, ti, 0, 0)),
            pl.BlockSpec((None, cpt, 2 * CHUNK, RW_WIDTH), lambda bi, ti: (bi, ti, 0, 0)),
            pl.BlockSpec((None, tt, RW_WIDTH), lambda bi, ti: (bi, ti, 0)),
            pl.BlockSpec((None, tt, RW_WIDTH), lambda bi, ti: (bi, ti, 0)),
        ],
        out_shape=[
            jax.ShapeDtypeStruct((b, nc, 2 * CHUNK, RW_WIDTH), BF16),
            jax.ShapeDtypeStruct((b, nc, 2 * CHUNK, RW_WIDTH), F32),
            jax.ShapeDtypeStruct((b, s, RW_WIDTH), F32),
            jax.ShapeDtypeStruct((b, s, RW_WIDTH), F32),
        ],
        compiler_params=pltpu.CompilerParams(
            dimension_semantics=("parallel", "parallel"), vmem_limit_bytes=VMEM_LIMIT_BYTES),
        name="rwkv_prep",
    )(rw, rw, mu, w0, a0, k_k, k_a, r_k, wa2, g2, ones_bd, tri)


def _rwkv_scan_kernel(rm_ref, yg_ref, bonus_ref, g_ref, lng_ref, lnb_ref, ones_ref,
                      y_ref, h_sc, y_sc):
    c = pl.program_id(0)
    nb = rm_ref.shape[0]

    @pl.when(c == 0)
    def _():
        h_sc[...] = jnp.zeros(h_sc.shape, F32)

    r128 = lax.broadcasted_iota(jnp.int32, (2 * CHUNK, LANES), 0)
    l128 = lax.broadcasted_iota(jnp.int32, (2 * CHUNK, LANES), 1)
    bd_mask = (r128 < CHUNK) == (l128 < RW_HEAD)

    def body(bi, carry):
        for p in range(RW_PAIRS):
            lanes = slice(p * LANES, (p + 1) * LANES)
            h_bd = h_sc[bi * RW_PAIRS + p]
            out = _dot(rm_ref[bi, :, lanes], h_bd.astype(BF16)) + yg_ref[bi, :, lanes]
            y_sc[bi, :, lanes] = out[:CHUNK]
            h_new = out[CHUNK:]
            h_sc[bi * RW_PAIRS + p] = jnp.where(
                bd_mask, jnp.concatenate([h_new, h_new], axis=0), 0.0)
        return carry

    lax.fori_loop(0, nb, body, 0)

    ones_bd = ones_ref[...]
    y = y_sc[...].reshape(nb * CHUNK, RW_WIDTH)
    mean = _segsum(y, ones_bd) * (1.0 / RW_HEAD)
    d = y - mean
    var = _segsum(d * d, ones_bd) * (1.0 / RW_HEAD)
    yn = d * lax.rsqrt(var + RW_GN_EPS) * lng_ref[...] + lnb_ref[...]
    bonus = bonus_ref[...].reshape(nb * CHUNK, RW_WIDTH)
    gate = g_ref[...].reshape(nb * CHUNK, RW_WIDTH)
    y_ref[...] = ((yn + bonus) * gate).reshape(nb, CHUNK, RW_WIDTH).astype(y_ref.dtype)


def _rwkv_scan(rm, yg, bonus, g, ln_g, ln_b, ones_bd):
    b, nc, _, _ = rm.shape
    s = nc * CHUNK
    vec = lambda n: pl.BlockSpec((1, n), lambda ci: (0, 0))
    return pl.pallas_call(
        _rwkv_scan_kernel,
        grid=(nc,),
        in_specs=[
            pl.BlockSpec((b, None, 2 * CHUNK, RW_WIDTH), lambda ci: (0, ci, 0, 0)),
            pl.BlockSpec((b, None, 2 * CHUNK, RW_WIDTH), lambda ci: (0, ci, 0, 0)),
            pl.BlockSpec((b, CHUNK, RW_WIDTH), lambda ci: (0, ci, 0)),
            pl.BlockSpec((b, CHUNK, RW_WIDTH), lambda ci: (0, ci, 0)),
            vec(RW_WIDTH), vec(RW_WIDTH),
            pl.BlockSpec((RW_WIDTH, RW_WIDTH), lambda ci: (0, 0)),
        ],
        out_specs=pl.BlockSpec((b, CHUNK, RW_WIDTH), lambda ci: (0, ci, 0)),
        out_shape=jax.ShapeDtypeStruct((b, s, RW_WIDTH), BF16),
        scratch_shapes=[
            pltpu.VMEM((b * RW_PAIRS, 2 * CHUNK, LANES), F32),
            pltpu.VMEM((b, CHUNK, RW_WIDTH), F32),
        ],
        compiler_params=pltpu.CompilerParams(
            dimension_semantics=("arbitrary",), vmem_limit_bytes=VMEM_LIMIT_BYTES),
        name="rwkv_scan",
    )(rm, yg, bonus, g, ln_g, ln_b, ones_bd)


def _merge_ffn_kernel(x_ref, ya_ref, yb_ref, gate_ref, wa_ref, wb_ref, wo_ref,
                      gffn_ref, w1_ref, w2_ref, gfin_ref, o_ref):
    pa = _dot(ya_ref[...], wa_ref[...])
    pb = _dot(yb_ref[...], wb_ref[...])
    gate = gate_ref[...]
    merged = (jax.nn.sigmoid(gate[:, :D_MODEL]) * pa
              + jax.nn.sigmoid(gate[:, D_MODEL:]) * pb)
    x1 = x_ref[...] + _dot(merged.astype(BF16), wo_ref[...])
    ms = jnp.mean(x1 * x1, axis=-1, keepdims=True)
    h = (x1 * lax.rsqrt(ms + NORM_EPS) * gffn_ref[...]).astype(BF16)
    acc = x1
    for c in range(D_FF // FFN_CHUNK):
        cols = slice(c * FFN_CHUNK, (c + 1) * FFN_CHUNK)
        f = jnp.maximum(_dot(h, w1_ref[:, cols]), 0.0)
        acc = acc + _dot((f * f).astype(BF16), w2_ref[cols, :])
    ms2 = jnp.mean(acc * acc, axis=-1, keepdims=True)
    o_ref[...] = acc * lax.rsqrt(ms2 + NORM_EPS) * gfin_ref[...]


def _merge_ffn(x2d, ya, yb, gates, wa, wb, wo, gffn, w1, w2, gfin):
    m = x2d.shape[0]
    tm = FFN_ROWS
    row = lambda i: (i, 0)
    return pl.pallas_call(
        _merge_ffn_kernel,
        grid=(m // tm,),
        in_specs=[
            pl.BlockSpec((tm, D_MODEL), row),
            pl.BlockSpec((tm, DA_WIDTH), row),
            pl.BlockSpec((tm, RW_WIDTH), row),
            pl.BlockSpec((tm, GATE_WIDTH), row),
            _resident((DA_WIDTH, D_MODEL)),
            _resident((RW_WIDTH, D_MODEL)),
            _resident((D_MODEL, D_MODEL)),
            _resident((1, D_MODEL)),
            _resident((D_MODEL, D_FF)),
            _resident((D_FF, D_MODEL)),
            _resident((1, D_MODEL)),
        ],
        out_specs=pl.BlockSpec((tm, D_MODEL), row),
        out_shape=jax.ShapeDtypeStruct((m, D_MODEL), F32),
        compiler_params=pltpu.CompilerParams(
            dimension_semantics=("parallel",), vmem_limit_bytes=VMEM_LIMIT_BYTES),
        name="merge_ffn",
    )(x2d, ya, yb, gates, wa, wb, wo, gffn, w1, w2, gfin)


def _rope_tables(seq):
    d = DA_HALF_DIM
    pos = jnp.arange(seq, dtype=F32)
    inv_freq = ROPE_THETA ** (-jnp.arange(0, d, 2, dtype=F32) / d)
    ang = pos[:, None] * inv_freq[None, :]
    cos = jnp.cos(ang)
    sin = jnp.sin(ang)
    reps = DA_WIDTH // d
    cos_full = jnp.tile(jnp.concatenate([cos, cos], axis=-1), (1, reps))
    sin_full = jnp.tile(jnp.concatenate([-sin, sin], axis=-1), (1, reps))
    return cos_full, sin_full


def kernel(x, norm_mix_g, w_in, rw_mu, rw_w0, rw_w2, rw_a0, rw_a2, rw_g2, rw_k_k, rw_k_a,
           rw_r_k, rw_ln_g, rw_ln_b, da_lq1, da_lk1, da_lq2, da_lk2, da_subln_g,
           w_branch_a, w_branch_b, w_o, norm_ffn_g, w_ff1, w_ff2, norm_final_g):
    b, s, d = x.shape
    assert d == D_MODEL and norm_mix_g.shape[0] == 1
    assert s % ATTN_BLOCK == 0 and s % INPROJ_ROWS == 0 and s % RW_PREP_ROWS == 0
    assert (b * s) % FFN_ROWS == 0
    row = lambda t: t.reshape(1, -1)
    x2d = x.reshape(b * s, d)

    cos, sin = _rope_tables(s)
    scale = DA_HALF_DIM ** -0.5
    q, k, v, rw, gates = _inproj(x2d, norm_mix_g, w_in[0].astype(BF16),
                                 cos * scale, sin * scale, cos, sin, s)

    ya = _attention(q.reshape(b, s, DA_WIDTH), k.reshape(b, s, DA_WIDTH),
                    v.reshape(b, s, DA_WIDTH), row(da_lq1), row(da_lk1), row(da_lq2),
                    row(da_lk2), row(da_subln_g))

    wa2 = jnp.zeros((LANES, 2 * RW_WIDTH), F32)
    wa2 = wa2.at[:RW_DECAY_LORA, :RW_WIDTH].set(rw_w2[0])
    wa2 = wa2.at[RW_DECAY_LORA:, RW_WIDTH:].set(rw_a2[0])
    idx = jnp.arange(RW_WIDTH) // RW_HEAD
    ones_bd = (idx[:, None] == idx[None, :]).astype(BF16)
    tok = jnp.arange(RW_PREP_ROWS)
    tri = ((tok[:, None] >= tok[None, :])
           & (tok[:, None] // CHUNK == tok[None, :] // CHUNK)).astype(BF16)

    rm, yg, bonus, g = _rwkv_prep(
        rw.reshape(b, s, RW_IN_WIDTH), rw_mu, rw_w0, rw_a0, rw_k_k, rw_k_a, row(rw_r_k),
        wa2.astype(BF16), rw_g2[0].astype(BF16), ones_bd, tri)
    yb = _rwkv_scan(rm, yg, bonus, g, rw_ln_g, rw_ln_b, ones_bd)

    out = _merge_ffn(x2d, ya.reshape(b * s, DA_WIDTH), yb.reshape(b * s, RW_WIDTH), gates,
                     w_branch_a[0].astype(BF16), w_branch_b[0].astype(BF16),
                     w_o[0].astype(BF16), norm_ffn_g, w_ff1[0].astype(BF16),
                     w_ff2[0].astype(BF16), row(norm_final_g))
    return out.reshape(b, s, d)
```

```python
import functools
import math

import jax
import jax.numpy as jnp
from jax import lax
from jax.experimental import pallas as pl
from jax.experimental.pallas import tpu as pltpu

F32 = jnp.float32
BF16 = jnp.bfloat16

D_MODEL = 1024
CHUNK = 64
NORM_EPS = 1e-6
ROPE_THETA = 10000.0
DA_WIDTH = 512
DA_HEADS = 4
DA_HALF_DIM = 64
DA_VDIM = 128
DA_SUBLN_EPS = 1e-5
RW_WIDTH = 512
RW_HEAD = 64
RW_PAIRS = RW_WIDTH // (2 * RW_HEAD)
RW_DECAY_LORA = 64
RW_AAA_LORA = 64
RW_GATE_LORA = 128
RW_GN_EPS = RW_HEAD * 1e-5
RW_IN_WIDTH = 3 * RW_WIDTH + RW_DECAY_LORA + RW_AAA_LORA + RW_GATE_LORA
DA_IN_WIDTH = 3 * DA_WIDTH
GATE_WIDTH = 2 * D_MODEL
D_IN = DA_IN_WIDTH + RW_IN_WIDTH + GATE_WIDTH
D_FF = 4 * D_MODEL
LAMBDA_INIT = 0.8 - 0.6 * math.exp(0.0)

LANES = 128
MXU_TILE = 256
SEG_TILE = MXU_TILE
VMEM_LIMIT_BYTES = 56 * 1024 * 1024

INPROJ_ROWS = 512
ATTN_BLOCK = 256
ATTN_HEAD_GROUP = 4
RW_PREP_ROWS = 256
FFN_ROWS = 512
FFN_CHUNK = 1024


def _dot(a, b):
    return jnp.dot(a, b, preferred_element_type=F32)


def _dot_nt(a, b):
    return lax.dot_general(a, b, (((1,), (1,)), ((), ())), preferred_element_type=F32)


def _split2(x):
    hi = x.astype(BF16)
    lo = (x - hi.astype(F32)).astype(BF16)
    return hi, lo


def _segsum(x, ones_bd):
    hi, lo = _split2(x)
    w = ones_bd.shape[0]
    parts = [_dot(hi[:, c:c + w], ones_bd) + _dot(lo[:, c:c + w], ones_bd)
             for c in range(0, x.shape[1], w)]
    return jnp.concatenate(parts, axis=1)


def _resident(shape):
    nd = len(shape)
    return pl.BlockSpec(shape, lambda *_: (0,) * nd, pipeline_mode=pl.Buffered(1))


def _inproj_kernel(x_ref, g_ref, w_ref, cq_ref, sq_ref, ck_ref, sk_ref,
                   q_ref, k_ref, v_ref, rw_ref, gate_ref):
    x = x_ref[...]
    ms = jnp.mean(x * x, axis=-1, keepdims=True)
    h = (x * lax.rsqrt(ms + NORM_EPS) * g_ref[...]).astype(BF16)

    def proj(c0, c1):
        return _dot(h, w_ref[:, c0:c1])

    rows = x.shape[0]
    lane = lax.broadcasted_iota(jnp.int32, (rows, DA_WIDTH), 1)
    first_half = (lane & (DA_HALF_DIM - 1)) < (DA_HALF_DIM // 2)

    def rope(t, cos, sin):
        partner = jnp.where(first_half,
                            pltpu.roll(t, DA_WIDTH - DA_HALF_DIM // 2, 1),
                            pltpu.roll(t, DA_HALF_DIM // 2, 1))
        return t * cos + partner * sin

    q_ref[...] = rope(proj(0, DA_WIDTH), cq_ref[...], sq_ref[...]).astype(q_ref.dtype)
    k_ref[...] = rope(proj(DA_WIDTH, 2 * DA_WIDTH), ck_ref[...], sk_ref[...]).astype(k_ref.dtype)
    v_ref[...] = proj(2 * DA_WIDTH, 3 * DA_WIDTH).astype(v_ref.dtype)
    rw_ref[...] = proj(DA_IN_WIDTH, DA_IN_WIDTH + RW_IN_WIDTH)
    gate_ref[...] = proj(DA_IN_WIDTH + RW_IN_WIDTH, D_IN)


def _inproj(x2d, g, w_bf, cq, sq, ck, sk, seq):
    m = x2d.shape[0]
    tm = INPROJ_ROWS
    nseq = seq // tm
    row = lambda i: (i, 0)
    tab = lambda i: (i % nseq, 0)
    return pl.pallas_call(
        _inproj_kernel,
        grid=(m // tm,),
        in_specs=[
            pl.BlockSpec((tm, D_MODEL), row),
            _resident((1, D_MODEL)),
            _resident((D_MODEL, D_IN)),
            pl.BlockSpec((tm, DA_WIDTH), tab),
            pl.BlockSpec((tm, DA_WIDTH), tab),
            pl.BlockSpec((tm, DA_WIDTH), tab),
            pl.BlockSpec((tm, DA_WIDTH), tab),
        ],
        out_specs=[
            pl.BlockSpec((tm, DA_WIDTH), row),
            pl.BlockSpec((tm, DA_WIDTH), row),
            pl.BlockSpec((tm, DA_WIDTH), row),
            pl.BlockSpec((tm, RW_IN_WIDTH), row),
            pl.BlockSpec((tm, GATE_WIDTH), row),
        ],
        out_shape=[
            jax.ShapeDtypeStruct((m, DA_WIDTH), BF16),
            jax.ShapeDtypeStruct((m, DA_WIDTH), BF16),
            jax.ShapeDtypeStruct((m, DA_WIDTH), BF16),
            jax.ShapeDtypeStruct((m, RW_IN_WIDTH), F32),
            jax.ShapeDtypeStruct((m, GATE_WIDTH), F32),
        ],
        compiler_params=pltpu.CompilerParams(
            dimension_semantics=("parallel",), vmem_limit_bytes=VMEM_LIMIT_BYTES),
        name="inproj",
    )(x2d, g, w_bf, cq, sq, ck, sk)


def _attn_kernel(lq1_ref, lk1_ref, lq2_ref, lk2_ref, sg_ref, q_ref, k_ref, v_ref,
                 o_ref):
    tq = q_ref.shape[0]
    heads = q_ref.shape[1] // DA_VDIM
    i = pl.program_id(2)
    lane = lax.broadcasted_iota(jnp.int32, (tq, DA_VDIM), 1)
    lam = (jnp.exp(jnp.sum(lq1_ref[...] * lk1_ref[...], axis=-1, keepdims=True))
           - jnp.exp(jnp.sum(lq2_ref[...] * lk2_ref[...], axis=-1, keepdims=True))
           + LAMBDA_INIT)
    r = lax.broadcasted_iota(jnp.int32, (2 * tq, tq), 0)
    c = lax.broadcasted_iota(jnp.int32, (2 * tq, tq), 1)
    allowed = (c // CHUNK) <= ((r & (tq - 1)) // CHUNK)

    def attend(nfull, h):
        cols = slice(h * DA_VDIM, (h + 1) * DA_VDIM)
        q = q_ref[:, cols]
        zero = jnp.zeros_like(q)
        qq = jnp.concatenate([jnp.where(lane < DA_HALF_DIM, q, zero),
                              jnp.where(lane >= DA_HALF_DIM, q, zero)], axis=0)
        s = []
        for j in range(nfull + 1):
            s_j = _dot_nt(qq, k_ref[j * tq:(j + 1) * tq, cols])
            s.append(jnp.where(allowed, s_j, -jnp.inf) if j == nfull else s_j)
        m_lane = s[0]
        for s_j in s[1:]:
            m_lane = jnp.maximum(m_lane, s_j)
        m = jnp.max(m_lane, axis=-1, keepdims=True)
        l_lane = None
        acc = None
        for j, s_j in enumerate(s):
            p = jnp.exp(s_j - m)
            pv = _dot(p.astype(BF16), v_ref[j * tq:(j + 1) * tq, cols])
            l_lane = p if l_lane is None else l_lane + p
            acc = pv if acc is None else acc + pv
        l = jnp.sum(l_lane, axis=-1, keepdims=True)
        o = acc / l
        od = o[:tq] - lam * o[tq:]
        ms = jnp.mean(od * od, axis=-1, keepdims=True)
        y = od * lax.rsqrt(ms + DA_SUBLN_EPS) * sg_ref[...]
        o_ref[:, cols] = (y * (1.0 - LAMBDA_INIT)).astype(o_ref.dtype)

    def variant(nfull):
        for h in range(heads):
            attend(nfull, h)

    for n in range(k_ref.shape[0] // tq):
        pl.when(i == n)(functools.partial(variant, n))


def _attention(q, k, v, lq1, lk1, lq2, lk2, subln_g):
    b, s, _ = q.shape
    tq = ATTN_BLOCK
    assert tq & (tq - 1) == 0 and s % tq == 0
    vec = lambda n: pl.BlockSpec((1, n), lambda bi, hi, qi: (0, 0))
    gw = ATTN_HEAD_GROUP * DA_VDIM
    return pl.pallas_call(
        _attn_kernel,
        grid=(b, DA_HEADS // ATTN_HEAD_GROUP, s // tq),
        in_specs=[
            vec(DA_HALF_DIM), vec(DA_HALF_DIM), vec(DA_HALF_DIM), vec(DA_HALF_DIM),
            vec(DA_VDIM),
            pl.BlockSpec((None, tq, gw), lambda bi, hi, qi: (bi, qi, hi)),
            pl.BlockSpec((None, s, gw), lambda bi, hi, qi: (bi, 0, hi)),
            pl.BlockSpec((None, s, gw), lambda bi, hi, qi: (bi, 0, hi)),
        ],
        out_specs=pl.BlockSpec((None, tq, gw), lambda bi, hi, qi: (bi, qi, hi)),
        out_shape=jax.ShapeDtypeStruct((b, s, DA_WIDTH), BF16),
        compiler_params=pltpu.CompilerParams(
            dimension_semantics=("parallel", "parallel", "arbitrary"),
            vmem_limit_bytes=VMEM_LIMIT_BYTES),
        name="diff_attn",
    )(lq1, lk1, lq2, lk2, subln_g, q, k, v)


def _rwkv_prep_kernel(rw_ref, prev_ref, mu_ref, w0_ref, a0_ref, kk_ref, ka_ref, rk_ref,
                      wa2_ref, g2_ref, ones_ref, tri_ref,
                      rm_ref, yg_ref, bonus_ref, g_ref):
    i = pl.program_id(1)
    tt = rw_ref.shape[0]
    z = rw_ref[...]
    prev = jnp.where(i > 0, prev_ref[7:8, :], 0.0)
    row = lax.broadcasted_iota(jnp.int32, z.shape, 0)
    z_prev = jnp.where(row == 0, prev, pltpu.roll(z, 1, 0))
    zs = z + (z_prev - z) * mu_ref[...]

    w3 = 3 * RW_WIDTH
    r = zs[:, 0:RW_WIDTH]
    k = zs[:, RW_WIDTH:2 * RW_WIDTH]
    v = zs[:, 2 * RW_WIDTH:w3]
    x_wa = zs[:, w3:w3 + LANES]
    x_g = zs[:, w3 + LANES:w3 + 2 * LANES]
    lane = lax.broadcasted_iota(jnp.int32, x_wa.shape, 1)
    t_wa = jnp.where(lane < RW_DECAY_LORA, jnp.tanh(x_wa), x_wa)
    lora = _dot(t_wa.astype(BF16), wa2_ref[...])
    u = -(w0_ref[...] + lora[:, :RW_WIDTH])
    softplus = jnp.maximum(u, 0.0) + jnp.log1p(jnp.exp(-jnp.abs(u)))
    log_decay = -jnp.exp(-softplus - 0.5)
    a_lr = jax.nn.sigmoid(a0_ref[...] + lora[:, RW_WIDTH:])
    g_ref[...] = _dot(jax.nn.sigmoid(x_g).astype(BF16), g2_ref[...])

    ones_bd = ones_ref[...]
    kk = k * kk_ref[...]
    kk = kk / jnp.maximum(jnp.sqrt(_segsum(kk * kk, ones_bd)), 1e-12)
    k2 = k * (1.0 + (a_lr - 1.0) * ka_ref[...])
    a_vec = -kk
    b_vec = kk * a_lr
    bonus_ref[...] = _segsum(r * k2 * rk_ref[...], ones_bd) * v

    tri = tri_ref[...]
    h1 = log_decay.astype(BF16)
    r1 = log_decay - h1.astype(F32)
    h2 = r1.astype(BF16)
    h3 = (r1 - h2.astype(F32)).astype(BF16)
    cs = _dot(tri, h1) + _dot(tri, h2) + _dot(tri, h3)
    nchunk = tt // CHUNK
    cl = jnp.concatenate(
        [jnp.broadcast_to(cs[(c + 1) * CHUNK - 1:(c + 1) * CHUNK, :], (CHUNK, RW_WIDTH))
         for c in range(nchunk)], axis=0)
    gam_inv = jnp.exp(-cs)
    gam_end = jnp.exp(cl - cs)
    r_t = r * jnp.exp(cs)
    a_t = a_vec * jnp.exp(cs - log_decay)
    k_t = k2 * gam_inv
    b_t = b_vec * gam_inv
    k_h = k2 * gam_end
    b_h = b_vec * gam_end
    gam_c = jnp.exp(cl)

    quad = 2 * LANES
    cq = (CHUNK, quad)
    t_idx = lax.broadcasted_iota(jnp.int32, cq, 0)
    l_idx = lax.broadcasted_iota(jnp.int32, cq, 1)
    s_idx = l_idx & (RW_HEAD - 1)
    strict = s_idx < t_idx
    incl = s_idx <= t_idx
    diag = s_idx == t_idx
    even_head = (l_idx & RW_HEAD) == 0
    r4 = lax.broadcasted_iota(jnp.int32, (quad, quad), 0)
    l4 = lax.broadcasted_iota(jnp.int32, (quad, quad), 1)
    bd4_mask = (r4 // RW_HEAD) == (l4 // RW_HEAD)
    r2 = lax.broadcasted_iota(jnp.int32, (2 * CHUNK, LANES), 0)
    l2 = lax.broadcasted_iota(jnp.int32, (2 * CHUNK, LANES), 1)
    bd_mask = (r2 < CHUNK) == (l2 < RW_HEAD)
    eye = (r2 == l2).astype(F32)
    eye2 = jnp.concatenate([eye, eye], axis=0)
    z_pair = jnp.zeros((2 * CHUNK, LANES), F32)
    z_half = jnp.zeros((CHUNK, LANES), F32)

    def bd(x):
        return jnp.where(bd_mask, jnp.concatenate([x, x], axis=0), 0.0)

    def bd4(x):
        return jnp.where(bd4_mask, jnp.concatenate([x, x, x, x], axis=0), 0.0)

    def blockdiag2(xa, xb):
        return jnp.concatenate([jnp.concatenate([xa, z_pair], axis=1),
                                jnp.concatenate([z_pair, xb], axis=1)], axis=0)

    def stacked_bd(st):
        return blockdiag2(st[:2 * CHUNK], st[2 * CHUNK:])

    items = [(c, qd) for c in range(nchunk) for qd in range(RW_WIDTH // quad)]

    def view(x, item):
        c, qd = item
        return x[c * CHUNK:(c + 1) * CHUNK, qd * quad:(qd + 1) * quad]

    a2 = [_dot_nt(jnp.concatenate([view(a_t, it), view(r_t, it)], axis=0).astype(BF16),
                  jnp.concatenate([bd4(view(b_t, it)), bd4(view(k_t, it))], axis=0).astype(BF16))
          for it in items]
    a_ab = [jnp.where(strict, z[:CHUNK, :quad], 0.0) for z in a2]
    a_ak = [jnp.where(strict, z[:CHUNK, quad:], 0.0) for z in a2]
    a_rb = [jnp.where(incl, z[CHUNK:, :quad], 0.0) for z in a2]
    a_rk = [jnp.where(incl, z[CHUNK:, quad:], 0.0) for z in a2]

    n_st = [jnp.concatenate([bd(z[:, :LANES]), bd(z[:, LANES:])], axis=0) for z in a_ab]
    t_st = [eye2 + n for n in n_st]
    p_st = [_dot(stacked_bd(n).astype(BF16), n.astype(BF16)) for n in n_st]
    for _ in range(4):
        zz = [_dot(stacked_bd(p).astype(BF16), jnp.concatenate([p, t], axis=1).astype(BF16))
              for p, t in zip(p_st, t_st)]
        p_st = [z[:, :LANES] for z in zz]
        t_st = [t + z[:, LANES:] for t, z in zip(t_st, zz)]
    t_st = [t + _dot(stacked_bd(p).astype(BF16), t.astype(BF16)) for p, t in zip(p_st, t_st)]

    bd4_v = [bd4(view(v, it)) for it in items]
    av = [_dot(x.astype(BF16), bv.astype(BF16)) for x, bv in zip(a_ak, bd4_v)]
    x_st = [jnp.concatenate(
        [jnp.concatenate([bd(view(a_t, it)[:, :LANES]), bd(u[:, :LANES])], axis=1),
         jnp.concatenate([bd(view(a_t, it)[:, LANES:]), bd(u[:, LANES:])], axis=1)], axis=0)
        for it, u in zip(items, av)]
    tx = [_dot(stacked_bd(t).astype(BF16), x.astype(BF16)) for t, x in zip(t_st, x_st)]

    for it, tx_i, arb, ark, bv4 in zip(items, tx, a_rb, a_rk, bd4_v):
        c, qd = it
        r_bar, y_bar, au = [], [], []
        for half in range(2):
            pl_ = slice(half * LANES, (half + 1) * LANES)
            rows = slice(half * 2 * CHUNK, (half + 1) * 2 * CHUNK)
            tx_p = tx_i[rows]
            rhs = jnp.concatenate(
                [tx_p, jnp.concatenate([z_pair, bv4[rows, pl_]], axis=1)], axis=0)
            ry = _dot(jnp.concatenate([arb[:, pl_], ark[:, pl_]], axis=1).astype(BF16),
                      rhs.astype(BF16))
            r_bar.append(view(r_t, it)[:, pl_] + ry[:, :LANES])
            y_bar.append(ry[:, LANES:])
            au.append(tx_p[:CHUNK] + tx_p[CHUNK:])
        v_q = view(v, it)
        mg_rhs = jnp.concatenate(
            [au[0], jnp.concatenate([z_half, v_q[:, :LANES]], axis=1),
             au[1], jnp.concatenate([z_half, v_q[:, LANES:]], axis=1)], axis=0)
        lhs_t = jnp.concatenate([view(b_h, it), view(k_h, it)], axis=0).T
        mg = _dot(stacked_bd(lhs_t).astype(BF16), mg_rhs.astype(BF16))
        m_sel = jnp.concatenate([mg[0:CHUNK], mg[2 * CHUNK:3 * CHUNK]], axis=1)
        m_odd = jnp.concatenate([mg[CHUNK:2 * CHUNK], mg[3 * CHUNK:]], axis=1)
        m_even_odd = jnp.where(jnp.concatenate([even_head, even_head], axis=1), m_sel, m_odd)
        m_quad = (jnp.concatenate([m_even_odd[:, :LANES], m_even_odd[:, 2 * LANES:3 * LANES]], axis=1)
                  + jnp.where(diag, view(gam_c, it), 0.0))
        g_quad = jnp.concatenate([m_even_odd[:, LANES:2 * LANES], m_even_odd[:, 3 * LANES:]], axis=1)
        cols = slice(qd * quad, (qd + 1) * quad)
        rm_ref[c, :, cols] = jnp.concatenate(
            [jnp.concatenate(r_bar, axis=1), m_quad], axis=0).astype(rm_ref.dtype)
        yg_ref[c, :, cols] = jnp.concatenate([jnp.concatenate(y_bar, axis=1), g_quad], axis=0)


def _rwkv_prep(rw, mu, w0, a0, k_k, k_a, r_k, wa2, g2, ones_bd, tri):
    b, s, _ = rw.shape
    tt = RW_PREP_ROWS
    cpt = tt // CHUNK
    nc = s // CHUNK
    vec = lambda n: pl.BlockSpec((1, n), lambda bi, ti: (0, 0))
    return pl.pallas_call(
        _rwkv_prep_kernel,
        grid=(b, s // tt),
        in_specs=[
            pl.BlockSpec((None, tt, RW_IN_WIDTH), lambda bi, ti: (bi, ti, 0)),
            pl.BlockSpec((None, 8, RW_IN_WIDTH),
                         lambda bi, ti: (bi, jnp.maximum(ti * (tt // 8) - 1, 0), 0)),
            vec(RW_IN_WIDTH), vec(RW_WIDTH), vec(RW_WIDTH), vec(RW_WIDTH), vec(RW_WIDTH),
            vec(RW_WIDTH),
            pl.BlockSpec((LANES, 2 * RW_WIDTH), lambda bi, ti: (0, 0)),
            pl.BlockSpec((RW_GATE_LORA, RW_WIDTH), lambda bi, ti: (0, 0)),
            pl.BlockSpec((SEG_TILE, SEG_TILE), lambda bi, ti: (0, 0)),
            pl.BlockSpec((tt, tt), lambda bi, ti: (0, 0)),
        ],
        out_specs=[
            pl.BlockSpec((None, cpt, 2 * CHUNK, RW_WIDTH), lambda bi, ti: (bi, ti, 0, 0)),
            pl.BlockSpec((None, cpt, 2 * CHUNK, RW_WIDTH), lambda bi, ti: (bi, ti, 0, 0)),
            pl.BlockSpec((None, tt, RW_WIDTH), lambda bi, ti: (bi, ti, 0)),
            pl.BlockSpec((None, tt, RW_WIDTH), lambda bi, ti: (bi, ti, 0)),
        ],
        out_shape=[
            jax.ShapeDtypeStruct((b, nc, 2 * CHUNK, RW_WIDTH), BF16),
            jax.ShapeDtypeStruct((b, nc, 2 * CHUNK, RW_WIDTH), F32),
            jax.ShapeDtypeStruct((b, s, RW_WIDTH), F32),
            jax.ShapeDtypeStruct((b, s, RW_WIDTH), F32),
        ],
        compiler_params=pltpu.CompilerParams(
            dimension_semantics=("parallel", "parallel"), vmem_limit_bytes=VMEM_LIMIT_BYTES),
        name="rwkv_prep",
    )(rw, rw, mu, w0, a0, k_k, k_a, r_k, wa2, g2, ones_bd, tri)


def _rwkv_scan_kernel(rm_ref, yg_ref, bonus_ref, g_ref, lng_ref, lnb_ref, ones_ref,
                      y_ref, h_sc, y_sc):
    c = pl.program_id(0)
    nb = rm_ref.shape[0]

    @pl.when(c == 0)
    def _():
        h_sc[...] = jnp.zeros(h_sc.shape, F32)

    r128 = lax.broadcasted_iota(jnp.int32, (2 * CHUNK, LANES), 0)
    l128 = lax.broadcasted_iota(jnp.int32, (2 * CHUNK, LANES), 1)
    bd_mask = (r128 < CHUNK) == (l128 < RW_HEAD)

    for bi in range(nb):
        for p in range(RW_PAIRS):
            lanes = slice(p * LANES, (p + 1) * LANES)
            h_bd = h_sc[bi * RW_PAIRS + p]
            out = _dot(rm_ref[bi, :, lanes], h_bd.astype(BF16)) + yg_ref[bi, :, lanes]
            y_sc[bi, :, lanes] = out[:CHUNK]
            h_new = out[CHUNK:]
            h_sc[bi * RW_PAIRS + p] = jnp.where(
                bd_mask, jnp.concatenate([h_new, h_new], axis=0), 0.0)

    ones_bd = ones_ref[...]
    y = y_sc[...].reshape(nb * CHUNK, RW_WIDTH)
    mean = _segsum(y, ones_bd) * (1.0 / RW_HEAD)
    d = y - mean
    var = _segsum(d * d, ones_bd) * (1.0 / RW_HEAD)
    yn = d * lax.rsqrt(var + RW_GN_EPS) * lng_ref[...] + lnb_ref[...]
    bonus = bonus_ref[...].reshape(nb * CHUNK, RW_WIDTH)
    gate = g_ref[...].reshape(nb * CHUNK, RW_WIDTH)
    y_ref[...] = ((yn + bonus) * gate).reshape(nb, CHUNK, RW_WIDTH).astype(y_ref.dtype)


def _rwkv_scan(rm, yg, bonus, g, ln_g, ln_b, ones_bd):
    b, nc, _, _ = rm.shape
    s = nc * CHUNK
    vec = lambda n: pl.BlockSpec((1, n), lambda ci: (0, 0))
    return pl.pallas_call(
        _rwkv_scan_kernel,
        grid=(nc,),
        in_specs=[
            pl.BlockSpec((b, None, 2 * CHUNK, RW_WIDTH), lambda ci: (0, ci, 0, 0)),
            pl.BlockSpec((b, None, 2 * CHUNK, RW_WIDTH), lambda ci: (0, ci, 0, 0)),
            pl.BlockSpec((b, CHUNK, RW_WIDTH), lambda ci: (0, ci, 0)),
            pl.BlockSpec((b, CHUNK, RW_WIDTH), lambda ci: (0, ci, 0)),
            vec(RW_WIDTH), vec(RW_WIDTH),
            pl.BlockSpec((SEG_TILE, SEG_TILE), lambda ci: (0, 0)),
        ],
        out_specs=pl.BlockSpec((b, CHUNK, RW_WIDTH), lambda ci: (0, ci, 0)),
        out_shape=jax.ShapeDtypeStruct((b, s, RW_WIDTH), BF16),
        scratch_shapes=[
            pltpu.VMEM((b * RW_PAIRS, 2 * CHUNK, LANES), F32),
            pltpu.VMEM((b, CHUNK, RW_WIDTH), F32),
        ],
        compiler_params=pltpu.CompilerParams(
            dimension_semantics=("arbitrary",), vmem_limit_bytes=VMEM_LIMIT_BYTES),
        name="rwkv_scan",
    )(rm, yg, bonus, g, ln_g, ln_b, ones_bd)


def _merge_ffn_kernel(x_ref, ya_ref, yb_ref, gate_ref, wa_ref, wb_ref, wo_ref,
                      gffn_ref, w1_ref, w2_ref, gfin_ref, o_ref):
    pa = _dot(ya_ref[...], wa_ref[...])
    pb = _dot(yb_ref[...], wb_ref[...])
    gate = gate_ref[...]
    merged = (jax.nn.sigmoid(gate[:, :D_MODEL]) * pa
              + jax.nn.sigmoid(gate[:, D_MODEL:]) * pb)
    x1 = x_ref[...] + _dot(merged.astype(BF16), wo_ref[...])
    ms = jnp.mean(x1 * x1, axis=-1, keepdims=True)
    h = (x1 * lax.rsqrt(ms + NORM_EPS) * gffn_ref[...]).astype(BF16)
    acc = x1
    for c in range(D_FF // FFN_CHUNK):
        cols = slice(c * FFN_CHUNK, (c + 1) * FFN_CHUNK)
        f = jnp.maximum(_dot(h, w1_ref[:, cols]), 0.0)
        acc = acc + _dot((f * f).astype(BF16), w2_ref[cols, :])
    ms2 = jnp.mean(acc * acc, axis=-1, keepdims=True)
    o_ref[...] = acc * lax.rsqrt(ms2 + NORM_EPS) * gfin_ref[...]


def _merge_ffn(x2d, ya, yb, gates, wa, wb, wo, gffn, w1, w2, gfin):
    m = x2d.shape[0]
    tm = FFN_ROWS
    row = lambda i: (i, 0)
    return pl.pallas_call(
        _merge_ffn_kernel,
        grid=(m // tm,),
        in_specs=[
            pl.BlockSpec((tm, D_MODEL), row),
            pl.BlockSpec((tm, DA_WIDTH), row),
            pl.BlockSpec((tm, RW_WIDTH), row),
            pl.BlockSpec((tm, GATE_WIDTH), row),
            _resident((DA_WIDTH, D_MODEL)),
            _resident((RW_WIDTH, D_MODEL)),
            _resident((D_MODEL, D_MODEL)),
            _resident((1, D_MODEL)),
            _resident((D_MODEL, D_FF)),
            _resident((D_FF, D_MODEL)),
            _resident((1, D_MODEL)),
        ],
        out_specs=pl.BlockSpec((tm, D_MODEL), row),
        out_shape=jax.ShapeDtypeStruct((m, D_MODEL), F32),
        compiler_params=pltpu.CompilerParams(
            dimension_semantics=("parallel",), vmem_limit_bytes=VMEM_LIMIT_BYTES),
        name="merge_ffn",
    )(x2d, ya, yb, gates, wa, wb, wo, gffn, w1, w2, gfin)


def _rope_tables(seq):
    d = DA_HALF_DIM
    pos = jnp.arange(seq, dtype=F32)
    inv_freq = ROPE_THETA ** (-jnp.arange(0, d, 2, dtype=F32) / d)
    ang = pos[:, None] * inv_freq[None, :]
    cos = jnp.cos(ang)
    sin = jnp.sin(ang)
    reps = DA_WIDTH // d
    cos_full = jnp.tile(jnp.concatenate([cos, cos], axis=-1), (1, reps))
    sin_full = jnp.tile(jnp.concatenate([-sin, sin], axis=-1), (1, reps))
    return cos_full, sin_full


def kernel(x, norm_mix_g, w_in, rw_mu, rw_w0, rw_w2, rw_a0, rw_a2, rw_g2, rw_k_k, rw_k_a,
           rw_r_k, rw_ln_g, rw_ln_b, da_lq1, da_lk1, da_lq2, da_lk2, da_subln_g,
           w_branch_a, w_branch_b, w_o, norm_ffn_g, w_ff1, w_ff2, norm_final_g):
    b, s, d = x.shape
    assert d == D_MODEL and norm_mix_g.shape[0] == 1
    assert s % ATTN_BLOCK == 0 and s % INPROJ_ROWS == 0 and s % RW_PREP_ROWS == 0
    assert (b * s) % FFN_ROWS == 0
    row = lambda t: t.reshape(1, -1)
    x2d = x.reshape(b * s, d)

    cos, sin = _rope_tables(s)
    scale = DA_HALF_DIM ** -0.5
    q, k, v, rw, gates = _inproj(x2d, norm_mix_g, w_in[0].astype(BF16),
                                 cos * scale, sin * scale, cos, sin, s)

    ya = _attention(q.reshape(b, s, DA_WIDTH), k.reshape(b, s, DA_WIDTH),
                    v.reshape(b, s, DA_WIDTH), row(da_lq1), row(da_lk1), row(da_lq2),
                    row(da_lk2), row(da_subln_g))

    wa2 = jnp.zeros((LANES, 2 * RW_WIDTH), F32)
    wa2 = wa2.at[:RW_DECAY_LORA, :RW_WIDTH].set(rw_w2[0])
    wa2 = wa2.at[RW_DECAY_LORA:, RW_WIDTH:].set(rw_a2[0])
    idx = jnp.arange(SEG_TILE) // RW_HEAD
    ones_bd = (idx[:, None] == idx[None, :]).astype(BF16)
    tok = jnp.arange(RW_PREP_ROWS)
    tri = ((tok[:, None] >= tok[None, :])
           & (tok[:, None] // CHUNK == tok[None, :] // CHUNK)).astype(BF16)

    rm, yg, bonus, g = _rwkv_prep(
        rw.reshape(b, s, RW_IN_WIDTH), rw_mu, rw_w0, rw_a0, rw_k_k, rw_k_a, row(rw_r_k),
        wa2.astype(BF16), rw_g2[0].astype(BF16), ones_bd, tri)
    yb = _rwkv_scan(rm, yg, bonus, g, rw_ln_g, rw_ln_b, ones_bd)

    out = _merge_ffn(x2d, ya.reshape(b * s, DA_WIDTH), yb.reshape(b * s, RW_WIDTH), gates,
                     w_branch_a[0].astype(BF16), w_branch_b[0].astype(BF16),
                     w_o[0].astype(BF16), norm_ffn_g, w_ff1[0].astype(BF16),
                     w_ff2[0].astype(BF16), row(norm_final_g))
    return out.reshape(b, s, d)
```

```python
import functools
import math

import jax
import jax.numpy as jnp
from jax import lax
from jax.experimental import pallas as pl
from jax.experimental.pallas import tpu as pltpu

F32 = jnp.float32
BF16 = jnp.bfloat16

D_MODEL = 1024
CHUNK = 64
NORM_EPS = 1e-6
ROPE_THETA = 10000.0
DA_WIDTH = 512
DA_HEADS = 4
DA_HALF_DIM = 64
DA_VDIM = 128
DA_SUBLN_EPS = 1e-5
RW_WIDTH = 512
RW_HEAD = 64
RW_PAIRS = RW_WIDTH // (2 * RW_HEAD)
RW_DECAY_LORA = 64
RW_AAA_LORA = 64
RW_GATE_LORA = 128
RW_GN_EPS = RW_HEAD * 1e-5
RW_IN_WIDTH = 3 * RW_WIDTH + RW_DECAY_LORA + RW_AAA_LORA + RW_GATE_LORA
DA_IN_WIDTH = 3 * DA_WIDTH
GATE_WIDTH = 2 * D_MODEL
D_IN = DA_IN_WIDTH + RW_IN_WIDTH + GATE_WIDTH
D_FF = 4 * D_MODEL
LAMBDA_INIT = 0.8 - 0.6 * math.exp(0.0)

LANES = 128
MXU_TILE = 256
SEG_TILE = MXU_TILE
VMEM_LIMIT_BYTES = 56 * 1024 * 1024

INPROJ_ROWS = 512
ATTN_BLOCK = 256
ATTN_HEAD_GROUP = 4
RW_PREP_ROWS = 256
FFN_ROWS = 512
FFN_CHUNK = 1024


def _dot(a, b):
    return jnp.dot(a, b, preferred_element_type=F32)


def _dot_nt(a, b):
    return lax.dot_general(a, b, (((1,), (1,)), ((), ())), preferred_element_type=F32)


def _split2(x):
    hi = x.astype(BF16)
    lo = (x - hi.astype(F32)).astype(BF16)
    return hi, lo


def _segsum(x, ones_bd):
    hi, lo = _split2(x)
    w = ones_bd.shape[0]
    parts = [_dot(hi[:, c:c + w], ones_bd) + _dot(lo[:, c:c + w], ones_bd)
             for c in range(0, x.shape[1], w)]
    return jnp.concatenate(parts, axis=1)


def _resident(shape):
    nd = len(shape)
    return pl.BlockSpec(shape, lambda *_: (0,) * nd, pipeline_mode=pl.Buffered(1))


def _inproj_kernel(x_ref, g_ref, w_ref, cq_ref, sq_ref, ck_ref, sk_ref,
                   q_ref, k_ref, v_ref, rw_ref, gate_ref):
    x = x_ref[...]
    ms = jnp.mean(x * x, axis=-1, keepdims=True)
    h = (x * lax.rsqrt(ms + NORM_EPS) * g_ref[...]).astype(BF16)

    def proj(c0, c1):
        return _dot(h, w_ref[:, c0:c1])

    rows = x.shape[0]
    lane = lax.broadcasted_iota(jnp.int32, (rows, DA_WIDTH), 1)
    first_half = (lane & (DA_HALF_DIM - 1)) < (DA_HALF_DIM // 2)

    def rope(t, cos, sin):
        partner = jnp.where(first_half,
                            pltpu.roll(t, DA_WIDTH - DA_HALF_DIM // 2, 1),
                            pltpu.roll(t, DA_HALF_DIM // 2, 1))
        return t * cos + partner * sin

    q_ref[...] = rope(proj(0, DA_WIDTH), cq_ref[...], sq_ref[...]).astype(q_ref.dtype)
    k_ref[...] = rope(proj(DA_WIDTH, 2 * DA_WIDTH), ck_ref[...], sk_ref[...]).astype(k_ref.dtype)
    v_ref[...] = proj(2 * DA_WIDTH, 3 * DA_WIDTH).astype(v_ref.dtype)
    rw_ref[...] = proj(DA_IN_WIDTH, DA_IN_WIDTH + RW_IN_WIDTH)
    gate_ref[...] = proj(DA_IN_WIDTH + RW_IN_WIDTH, D_IN)


def _inproj(x2d, g, w_bf, cq, sq, ck, sk, seq):
    m = x2d.shape[0]
    tm = INPROJ_ROWS
    nseq = seq // tm
    row = lambda i: (i, 0)
    tab = lambda i: (i % nseq, 0)
    return pl.pallas_call(
        _inproj_kernel,
        grid=(m // tm,),
        in_specs=[
            pl.BlockSpec((tm, D_MODEL), row),
            _resident((1, D_MODEL)),
            _resident((D_MODEL, D_IN)),
            pl.BlockSpec((tm, DA_WIDTH), tab),
            pl.BlockSpec((tm, DA_WIDTH), tab),
            pl.BlockSpec((tm, DA_WIDTH), tab),
            pl.BlockSpec((tm, DA_WIDTH), tab),
        ],
        out_specs=[
            pl.BlockSpec((tm, DA_WIDTH), row),
            pl.BlockSpec((tm, DA_WIDTH), row),
            pl.BlockSpec((tm, DA_WIDTH), row),
            pl.BlockSpec((tm, RW_IN_WIDTH), row),
            pl.BlockSpec((tm, GATE_WIDTH), row),
        ],
        out_shape=[
            jax.ShapeDtypeStruct((m, DA_WIDTH), BF16),
            jax.ShapeDtypeStruct((m, DA_WIDTH), BF16),
            jax.ShapeDtypeStruct((m, DA_WIDTH), BF16),
            jax.ShapeDtypeStruct((m, RW_IN_WIDTH), F32),
            jax.ShapeDtypeStruct((m, GATE_WIDTH), F32),
        ],
        compiler_params=pltpu.CompilerParams(
            dimension_semantics=("parallel",), vmem_limit_bytes=VMEM_LIMIT_BYTES),
        name="inproj",
    )(x2d, g, w_bf, cq, sq, ck, sk)


def _attn_kernel(lq1_ref, lk1_ref, lq2_ref, lk2_ref, sg_ref, q_ref, k_ref, v_ref,
                 o_ref):
    tq = q_ref.shape[0]
    heads = q_ref.shape[1] // DA_VDIM
    i = pl.program_id(2)
    lane = lax.broadcasted_iota(jnp.int32, (tq, DA_VDIM), 1)
    lam = (jnp.exp(jnp.sum(lq1_ref[...] * lk1_ref[...], axis=-1, keepdims=True))
           - jnp.exp(jnp.sum(lq2_ref[...] * lk2_ref[...], axis=-1, keepdims=True))
           + LAMBDA_INIT)
    r = lax.broadcasted_iota(jnp.int32, (2 * tq, tq), 0)
    c = lax.broadcasted_iota(jnp.int32, (2 * tq, tq), 1)
    allowed = (c // CHUNK) <= ((r & (tq - 1)) // CHUNK)

    def attend(nfull, h):
        cols = slice(h * DA_VDIM, (h + 1) * DA_VDIM)
        q = q_ref[:, cols]
        zero = jnp.zeros_like(q)
        qq = jnp.concatenate([jnp.where(lane < DA_HALF_DIM, q, zero),
                              jnp.where(lane >= DA_HALF_DIM, q, zero)], axis=0)
        s = []
        for j in range(nfull + 1):
            s_j = _dot_nt(qq, k_ref[j * tq:(j + 1) * tq, cols])
            s.append(jnp.where(allowed, s_j, -jnp.inf) if j == nfull else s_j)
            yield
        m_lane = s[0]
        for s_j in s[1:]:
            m_lane = jnp.maximum(m_lane, s_j)
        m = jnp.max(m_lane, axis=-1, keepdims=True)
        l_lane = None
        acc = None
        for j, s_j in enumerate(s):
            p = jnp.exp(s_j - m)
            pv = _dot(p.astype(BF16), v_ref[j * tq:(j + 1) * tq, cols])
            l_lane = p if l_lane is None else l_lane + p
            acc = pv if acc is None else acc + pv
            if j < nfull:
                yield
        l = jnp.sum(l_lane, axis=-1, keepdims=True)
        o = acc / l
        od = o[:tq] - lam * o[tq:]
        ms = jnp.mean(od * od, axis=-1, keepdims=True)
        y = od * lax.rsqrt(ms + DA_SUBLN_EPS) * sg_ref[...]
        o_ref[:, cols] = (y * (1.0 - LAMBDA_INIT)).astype(o_ref.dtype)

    def variant(nfull):
        gens = [attend(nfull, h) for h in range(heads)]
        nscore = nfull + 1
        for _ in range(nscore):
            next(gens[0])
        for h in range(heads):
            ahead = gens[h + 1] if h + 1 < heads else None
            left = nscore if ahead is not None else 0
            for _ in gens[h]:
                if left:
                    next(ahead)
                    left -= 1
            for _ in range(left):
                next(ahead)

    for n in range(k_ref.shape[0] // tq):
        pl.when(i == n)(functools.partial(variant, n))


def _attention(q, k, v, lq1, lk1, lq2, lk2, subln_g):
    b, s, _ = q.shape
    tq = ATTN_BLOCK
    assert tq & (tq - 1) == 0 and s % tq == 0
    vec = lambda n: pl.BlockSpec((1, n), lambda bi, hi, qi: (0, 0))
    gw = ATTN_HEAD_GROUP * DA_VDIM
    return pl.pallas_call(
        _attn_kernel,
        grid=(b, DA_HEADS // ATTN_HEAD_GROUP, s // tq),
        in_specs=[
            vec(DA_HALF_DIM), vec(DA_HALF_DIM), vec(DA_HALF_DIM), vec(DA_HALF_DIM),
            vec(DA_VDIM),
            pl.BlockSpec((None, tq, gw), lambda bi, hi, qi: (bi, qi, hi)),
            pl.BlockSpec((None, s, gw), lambda bi, hi, qi: (bi, 0, hi)),
            pl.BlockSpec((None, s, gw), lambda bi, hi, qi: (bi, 0, hi)),
        ],
        out_specs=pl.BlockSpec((None, tq, gw), lambda bi, hi, qi: (bi, qi, hi)),
        out_shape=jax.ShapeDtypeStruct((b, s, DA_WIDTH), BF16),
        compiler_params=pltpu.CompilerParams(
            dimension_semantics=("parallel", "parallel", "arbitrary"),
            vmem_limit_bytes=VMEM_LIMIT_BYTES),
        name="diff_attn",
    )(lq1, lk1, lq2, lk2, subln_g, q, k, v)


_STAGED = ("r_t", "a_t", "k_t", "b_t", "k_h", "b_h", "v", "gam_c")


def _rwkv_stage(seq_start, rw_ref, prev_ref, mu_ref, w0_ref, a0_ref, kk_ref, ka_ref, rk_ref,
                wa2_ref, g2_ref, ones_ref, tri_ref, bonus_ref, g_ref, stage_ref):
    tt = rw_ref.shape[0]
    z = rw_ref[...]
    prev = jnp.where(seq_start, 0.0, prev_ref[7:8, :])
    row = lax.broadcasted_iota(jnp.int32, z.shape, 0)
    z_prev = jnp.where(row == 0, prev, pltpu.roll(z, 1, 0))
    zs = z + (z_prev - z) * mu_ref[...]
    yield

    w3 = 3 * RW_WIDTH
    r = zs[:, 0:RW_WIDTH]
    k = zs[:, RW_WIDTH:2 * RW_WIDTH]
    v = zs[:, 2 * RW_WIDTH:w3]
    x_wa = zs[:, w3:w3 + LANES]
    x_g = zs[:, w3 + LANES:w3 + 2 * LANES]
    lane = lax.broadcasted_iota(jnp.int32, x_wa.shape, 1)
    t_wa = jnp.where(lane < RW_DECAY_LORA, jnp.tanh(x_wa), x_wa)
    lora = _dot(t_wa.astype(BF16), wa2_ref[...])
    u = -(w0_ref[...] + lora[:, :RW_WIDTH])
    softplus = jnp.maximum(u, 0.0) + jnp.log(1.0 + jnp.exp(-jnp.abs(u)))
    log_decay = -jnp.exp(-softplus - 0.5)
    yield
    a_lr = jax.nn.sigmoid(a0_ref[...] + lora[:, RW_WIDTH:])
    g_ref[...] = _dot(jax.nn.sigmoid(x_g).astype(BF16), g2_ref[...])
    yield

    ones_bd = ones_ref[...]
    kk = k * kk_ref[...]
    kk = kk * jnp.minimum(lax.rsqrt(_segsum(kk * kk, ones_bd)), 1e12)
    yield
    k2 = k * (1.0 + (a_lr - 1.0) * ka_ref[...])
    a_vec = -kk
    b_vec = kk * a_lr
    bonus_ref[...] = _segsum(r * k2 * rk_ref[...], ones_bd) * v
    yield

    tri = tri_ref[...]
    h1 = log_decay.astype(BF16)
    r1 = log_decay - h1.astype(F32)
    h2 = r1.astype(BF16)
    h3 = (r1 - h2.astype(F32)).astype(BF16)
    cs = _dot(tri, h1) + _dot(tri, h2) + _dot(tri, h3)
    nchunk = tt // CHUNK
    cl = jnp.concatenate(
        [jnp.broadcast_to(cs[(c + 1) * CHUNK - 1:(c + 1) * CHUNK, :], (CHUNK, RW_WIDTH))
         for c in range(nchunk)], axis=0)
    yield
    stage_ref[_STAGED.index("v")] = v
    stage_ref[_STAGED.index("gam_c")] = jnp.exp(cl)
    stage_ref[_STAGED.index("r_t")] = r * jnp.exp(cs)
    yield
    stage_ref[_STAGED.index("a_t")] = a_vec * jnp.exp(cs - log_decay)
    yield
    gam_inv = jnp.exp(-cs)
    stage_ref[_STAGED.index("k_t")] = k2 * gam_inv
    stage_ref[_STAGED.index("b_t")] = b_vec * gam_inv
    yield
    gam_end = jnp.exp(cl - cs)
    stage_ref[_STAGED.index("k_h")] = k2 * gam_end
    stage_ref[_STAGED.index("b_h")] = b_vec * gam_end


def _rwkv_chain(stage_ref, rm_ref, yg_ref):
    r_t, a_t, k_t, b_t, k_h, b_h, v, gam_c = _STAGED
    tt = stage_ref.shape[1]
    nchunk = tt // CHUNK

    quad = 2 * LANES
    cq = (CHUNK, quad)
    t_idx = lax.broadcasted_iota(jnp.int32, cq, 0)
    l_idx = lax.broadcasted_iota(jnp.int32, cq, 1)
    s_idx = l_idx & (RW_HEAD - 1)
    strict = s_idx < t_idx
    incl = s_idx <= t_idx
    diag = s_idx == t_idx
    even_head = (l_idx & RW_HEAD) == 0
    r4 = lax.broadcasted_iota(jnp.int32, (quad, quad), 0)
    l4 = lax.broadcasted_iota(jnp.int32, (quad, quad), 1)
    bd4_mask = (r4 // RW_HEAD) == (l4 // RW_HEAD)
    r2 = lax.broadcasted_iota(jnp.int32, (2 * CHUNK, LANES), 0)
    l2 = lax.broadcasted_iota(jnp.int32, (2 * CHUNK, LANES), 1)
    bd_mask = (r2 < CHUNK) == (l2 < RW_HEAD)
    eye = (r2 == l2).astype(F32)
    eye2 = jnp.concatenate([eye, eye], axis=0)
    z_pair = jnp.zeros((2 * CHUNK, LANES), F32)
    z_half = jnp.zeros((CHUNK, LANES), F32)

    def bd(x):
        return jnp.where(bd_mask, jnp.concatenate([x, x], axis=0), 0.0)

    def bd4(x):
        return jnp.where(bd4_mask, jnp.concatenate([x, x, x, x], axis=0), 0.0)

    def blockdiag2(xa, xb):
        return jnp.concatenate([jnp.concatenate([xa, z_pair], axis=1),
                                jnp.concatenate([z_pair, xb], axis=1)], axis=0)

    def stacked_bd(st):
        return blockdiag2(st[:2 * CHUNK], st[2 * CHUNK:])

    items = [(c, qd) for c in range(nchunk) for qd in range(RW_WIDTH // quad)]

    def view(name, item):
        c, qd = item
        return stage_ref[_STAGED.index(name), c * CHUNK:(c + 1) * CHUNK, qd * quad:(qd + 1) * quad]

    a2 = [_dot_nt(jnp.concatenate([view(a_t, it), view(r_t, it)], axis=0).astype(BF16),
                  jnp.concatenate([bd4(view(b_t, it)), bd4(view(k_t, it))], axis=0).astype(BF16))
          for it in items]
    yield
    a_ab = [jnp.where(strict, z[:CHUNK, :quad], 0.0) for z in a2]
    a_ak = [jnp.where(strict, z[:CHUNK, quad:], 0.0) for z in a2]
    a_rb = [jnp.where(incl, z[CHUNK:, :quad], 0.0) for z in a2]
    a_rk = [jnp.where(incl, z[CHUNK:, quad:], 0.0) for z in a2]

    n_st = [jnp.concatenate([bd(z[:, :LANES]), bd(z[:, LANES:])], axis=0) for z in a_ab]
    t_st = [eye2 + n for n in n_st]
    p_st = [_dot(stacked_bd(n).astype(BF16), n.astype(BF16)) for n in n_st]
    yield
    for _ in range(4):
        zz = [_dot(stacked_bd(p).astype(BF16), jnp.concatenate([p, t], axis=1).astype(BF16))
              for p, t in zip(p_st, t_st)]
        p_st = [z[:, :LANES] for z in zz]
        t_st = [t + z[:, LANES:] for t, z in zip(t_st, zz)]
        yield
    t_st = [t + _dot(stacked_bd(p).astype(BF16), t.astype(BF16)) for p, t in zip(p_st, t_st)]
    yield

    bd4_v = [bd4(view(v, it)) for it in items]
    av = [_dot(x.astype(BF16), bv.astype(BF16)) for x, bv in zip(a_ak, bd4_v)]
    x_st = [jnp.concatenate(
        [jnp.concatenate([bd(view(a_t, it)[:, :LANES]), bd(u[:, :LANES])], axis=1),
         jnp.concatenate([bd(view(a_t, it)[:, LANES:]), bd(u[:, LANES:])], axis=1)], axis=0)
        for it, u in zip(items, av)]
    tx = [_dot(stacked_bd(t).astype(BF16), x.astype(BF16)) for t, x in zip(t_st, x_st)]
    yield

    for n_item, (it, tx_i, arb, ark, bv4) in enumerate(zip(items, tx, a_rb, a_rk, bd4_v)):
        if n_item and n_item % 2 == 0:
            yield
        c, qd = it
        r_bar, y_bar, au = [], [], []
        for half in range(2):
            pl_ = slice(half * LANES, (half + 1) * LANES)
            rows = slice(half * 2 * CHUNK, (half + 1) * 2 * CHUNK)
            tx_p = tx_i[rows]
            rhs = jnp.concatenate(
                [tx_p, jnp.concatenate([z_pair, bv4[rows, pl_]], axis=1)], axis=0)
            ry = _dot(jnp.concatenate([arb[:, pl_], ark[:, pl_]], axis=1).astype(BF16),
                      rhs.astype(BF16))
            r_bar.append(view(r_t, it)[:, pl_] + ry[:, :LANES])
            y_bar.append(ry[:, LANES:])
            au.append(tx_p[:CHUNK] + tx_p[CHUNK:])
        v_q = view(v, it)
        mg_rhs = jnp.concatenate(
            [au[0], jnp.concatenate([z_half, v_q[:, :LANES]], axis=1),
             au[1], jnp.concatenate([z_half, v_q[:, LANES:]], axis=1)], axis=0)
        lhs_t = jnp.concatenate([view(b_h, it), view(k_h, it)], axis=0).T
        mg = _dot(stacked_bd(lhs_t).astype(BF16), mg_rhs.astype(BF16))
        m_sel = jnp.concatenate([mg[0:CHUNK], mg[2 * CHUNK:3 * CHUNK]], axis=1)
        m_odd = jnp.concatenate([mg[CHUNK:2 * CHUNK], mg[3 * CHUNK:]], axis=1)
        m_even_odd = jnp.where(jnp.concatenate([even_head, even_head], axis=1), m_sel, m_odd)
        m_quad = (jnp.concatenate([m_even_odd[:, :LANES], m_even_odd[:, 2 * LANES:3 * LANES]], axis=1)
                  + jnp.where(diag, view(gam_c, it), 0.0))
        g_quad = jnp.concatenate([m_even_odd[:, LANES:2 * LANES], m_even_odd[:, 3 * LANES:]], axis=1)
        cols = slice(qd * quad, (qd + 1) * quad)
        rm_ref[c, :, cols] = jnp.concatenate(
            [jnp.concatenate(r_bar, axis=1), m_quad], axis=0).astype(rm_ref.dtype)
        yg_ref[c, :, cols] = jnp.concatenate([jnp.concatenate(y_bar, axis=1), g_quad], axis=0)


def _rwkv_prep_kernel(rw_ref, prev_ref, mu_ref, w0_ref, a0_ref, kk_ref, ka_ref, rk_ref,
                      wa2_ref, g2_ref, ones_ref, tri_ref,
                      rm_ref, yg_ref, bonus_ref, g_ref, stage_a, stage_b,
                      *, ntiles, tiles_per_seq):
    t = pl.program_id(0)
    tile = jnp.minimum(t, ntiles - 1)
    seq_start = (tile % tiles_per_seq) == 0

    @pl.when(t == 0)
    def _():
        stage_b[...] = jnp.zeros(stage_b.shape, F32)

    def step(write_ref, read_ref):
        pending = [
            _rwkv_chain(read_ref, rm_ref, yg_ref),
            _rwkv_stage(seq_start, rw_ref, prev_ref, mu_ref, w0_ref, a0_ref, kk_ref, ka_ref,
                        rk_ref, wa2_ref, g2_ref, ones_ref, tri_ref, bonus_ref, g_ref, write_ref),
        ]
        while pending:
            for gen in list(pending):
                try:
                    next(gen)
                except StopIteration:
                    pending.remove(gen)

    pl.when(t % 2 == 0)(functools.partial(step, stage_a, stage_b))
    pl.when(t % 2 == 1)(functools.partial(step, stage_b, stage_a))


def _rwkv_prep(rw, mu, w0, a0, k_k, k_a, r_k, wa2, g2, ones_bd, tri, seq):
    m = rw.shape[0]
    tt = RW_PREP_ROWS
    cpt = tt // CHUNK
    ntiles = m // tt
    cur = lambda t: jnp.minimum(t, ntiles - 1)
    done = lambda t: jnp.maximum(t - 1, 0)
    vec = lambda n: pl.BlockSpec((1, n), lambda t: (0, 0))
    return pl.pallas_call(
        functools.partial(_rwkv_prep_kernel, ntiles=ntiles, tiles_per_seq=seq // tt),
        grid=(ntiles + 1,),
        in_specs=[
            pl.BlockSpec((tt, RW_IN_WIDTH), lambda t: (cur(t), 0)),
            pl.BlockSpec((8, RW_IN_WIDTH), lambda t: (jnp.maximum(cur(t) * (tt // 8) - 1, 0), 0)),
            vec(RW_IN_WIDTH), vec(RW_WIDTH), vec(RW_WIDTH), vec(RW_WIDTH), vec(RW_WIDTH),
            vec(RW_WIDTH),
            pl.BlockSpec((LANES, 2 * RW_WIDTH), lambda t: (0, 0)),
            pl.BlockSpec((RW_GATE_LORA, RW_WIDTH), lambda t: (0, 0)),
            pl.BlockSpec((SEG_TILE, SEG_TILE), lambda t: (0, 0)),
            pl.BlockSpec((tt, tt), lambda t: (0, 0)),
        ],
        out_specs=[
            pl.BlockSpec((cpt, 2 * CHUNK, RW_WIDTH), lambda t: (done(t), 0, 0)),
            pl.BlockSpec((cpt, 2 * CHUNK, RW_WIDTH), lambda t: (done(t), 0, 0)),
            pl.BlockSpec((tt, RW_WIDTH), lambda t: (cur(t), 0)),
            pl.BlockSpec((tt, RW_WIDTH), lambda t: (cur(t), 0)),
        ],
        out_shape=[
            jax.ShapeDtypeStruct((m // CHUNK, 2 * CHUNK, RW_WIDTH), BF16),
            jax.ShapeDtypeStruct((m // CHUNK, 2 * CHUNK, RW_WIDTH), F32),
            jax.ShapeDtypeStruct((m, RW_WIDTH), F32),
            jax.ShapeDtypeStruct((m, RW_WIDTH), F32),
        ],
        scratch_shapes=[
            pltpu.VMEM((len(_STAGED), tt, RW_WIDTH), F32),
            pltpu.VMEM((len(_STAGED), tt, RW_WIDTH), F32),
        ],
        compiler_params=pltpu.CompilerParams(
            dimension_semantics=("arbitrary",), vmem_limit_bytes=VMEM_LIMIT_BYTES),
        name="rwkv_prep",
    )(rw, rw, mu, w0, a0, k_k, k_a, r_k, wa2, g2, ones_bd, tri)


def _rwkv_scan_kernel(rm_ref, yg_ref, bonus_ref, g_ref, lng_ref, lnb_ref, ones_ref,
                      y_ref, h_sc, y_sc):
    c = pl.program_id(0)
    nb = rm_ref.shape[0]

    @pl.when(c == 0)
    def _():
        h_sc[...] = jnp.zeros(h_sc.shape, F32)

    r128 = lax.broadcasted_iota(jnp.int32, (2 * CHUNK, LANES), 0)
    l128 = lax.broadcasted_iota(jnp.int32, (2 * CHUNK, LANES), 1)
    bd_mask = (r128 < CHUNK) == (l128 < RW_HEAD)

    for bi in range(nb):
        for p in range(RW_PAIRS):
            lanes = slice(p * LANES, (p + 1) * LANES)
            h_bd = h_sc[bi * RW_PAIRS + p]
            out = _dot(rm_ref[bi, :, lanes], h_bd.astype(BF16)) + yg_ref[bi, :, lanes]
            y_sc[bi, :, lanes] = out[:CHUNK]
            h_new = out[CHUNK:]
            h_sc[bi * RW_PAIRS + p] = jnp.where(
                bd_mask, jnp.concatenate([h_new, h_new], axis=0), 0.0)

    ones_bd = ones_ref[...]
    y = y_sc[...].reshape(nb * CHUNK, RW_WIDTH)
    mean = _segsum(y, ones_bd) * (1.0 / RW_HEAD)
    d = y - mean
    var = _segsum(d * d, ones_bd) * (1.0 / RW_HEAD)
    yn = d * lax.rsqrt(var + RW_GN_EPS) * lng_ref[...] + lnb_ref[...]
    bonus = bonus_ref[...].reshape(nb * CHUNK, RW_WIDTH)
    gate = g_ref[...].reshape(nb * CHUNK, RW_WIDTH)
    y_ref[...] = ((yn + bonus) * gate).reshape(nb, CHUNK, RW_WIDTH).astype(y_ref.dtype)


def _rwkv_scan(rm, yg, bonus, g, ln_g, ln_b, ones_bd):
    b, nc, _, _ = rm.shape
    s = nc * CHUNK
    vec = lambda n: pl.BlockSpec((1, n), lambda ci: (0, 0))
    return pl.pallas_call(
        _rwkv_scan_kernel,
        grid=(nc,),
        in_specs=[
            pl.BlockSpec((b, None, 2 * CHUNK, RW_WIDTH), lambda ci: (0, ci, 0, 0)),
            pl.BlockSpec((b, None, 2 * CHUNK, RW_WIDTH), lambda ci: (0, ci, 0, 0)),
            pl.BlockSpec((b, CHUNK, RW_WIDTH), lambda ci: (0, ci, 0)),
            pl.BlockSpec((b, CHUNK, RW_WIDTH), lambda ci: (0, ci, 0)),
            vec(RW_WIDTH), vec(RW_WIDTH),
            pl.BlockSpec((SEG_TILE, SEG_TILE), lambda ci: (0, 0)),
        ],
        out_specs=pl.BlockSpec((b, CHUNK, RW_WIDTH), lambda ci: (0, ci, 0)),
        out_shape=jax.ShapeDtypeStruct((b, s, RW_WIDTH), BF16),
        scratch_shapes=[
            pltpu.VMEM((b * RW_PAIRS, 2 * CHUNK, LANES), F32),
            pltpu.VMEM((b, CHUNK, RW_WIDTH), F32),
        ],
        compiler_params=pltpu.CompilerParams(
            dimension_semantics=("arbitrary",), vmem_limit_bytes=VMEM_LIMIT_BYTES),
        name="rwkv_scan",
    )(rm, yg, bonus, g, ln_g, ln_b, ones_bd)


def _merge_ffn_kernel(x_ref, ya_ref, yb_ref, gate_ref, wa_ref, wb_ref, wo_ref,
                      gffn_ref, w1_ref, w2_ref, gfin_ref, o_ref):
    pa = _dot(ya_ref[...], wa_ref[...])
    pb = _dot(yb_ref[...], wb_ref[...])
    gate = gate_ref[...]
    merged = (jax.nn.sigmoid(gate[:, :D_MODEL]) * pa
              + jax.nn.sigmoid(gate[:, D_MODEL:]) * pb)
    x1 = x_ref[...] + _dot(merged.astype(BF16), wo_ref[...])
    ms = jnp.mean(x1 * x1, axis=-1, keepdims=True)
    h = (x1 * lax.rsqrt(ms + NORM_EPS) * gffn_ref[...]).astype(BF16)
    acc = x1
    for c in range(D_FF // FFN_CHUNK):
        cols = slice(c * FFN_CHUNK, (c + 1) * FFN_CHUNK)
        f = jnp.maximum(_dot(h, w1_ref[:, cols]), 0.0)
        acc = acc + _dot((f * f).astype(BF16), w2_ref[cols, :])
    ms2 = jnp.mean(acc * acc, axis=-1, keepdims=True)
    o_ref[...] = acc * lax.rsqrt(ms2 + NORM_EPS) * gfin_ref[...]


def _merge_ffn(x2d, ya, yb, gates, wa, wb, wo, gffn, w1, w2, gfin):
    m = x2d.shape[0]
    tm = FFN_ROWS
    row = lambda i: (i, 0)
    return pl.pallas_call(
        _merge_ffn_kernel,
        grid=(m // tm,),
        in_specs=[
            pl.BlockSpec((tm, D_MODEL), row),
            pl.BlockSpec((tm, DA_WIDTH), row),
            pl.BlockSpec((tm, RW_WIDTH), row),
            pl.BlockSpec((tm, GATE_WIDTH), row),
            _resident((DA_WIDTH, D_MODEL)),
            _resident((RW_WIDTH, D_MODEL)),
            _resident((D_MODEL, D_MODEL)),
            _resident((1, D_MODEL)),
            _resident((D_MODEL, D_FF)),
            _resident((D_FF, D_MODEL)),
            _resident((1, D_MODEL)),
        ],
        out_specs=pl.BlockSpec((tm, D_MODEL), row),
        out_shape=jax.ShapeDtypeStruct((m, D_MODEL), F32),
        compiler_params=pltpu.CompilerParams(
            dimension_semantics=("parallel",), vmem_limit_bytes=VMEM_LIMIT_BYTES),
        name="merge_ffn",
    )(x2d, ya, yb, gates, wa, wb, wo, gffn, w1, w2, gfin)


def _rope_tables(seq):
    d = DA_HALF_DIM
    pos = jnp.arange(seq, dtype=F32)
    inv_freq = ROPE_THETA ** (-jnp.arange(0, d, 2, dtype=F32) / d)
    ang = pos[:, None] * inv_freq[None, :]
    cos = jnp.cos(ang)
    sin = jnp.sin(ang)
    reps = DA_WIDTH // d
    cos_full = jnp.tile(jnp.concatenate([cos, cos], axis=-1), (1, reps))
    sin_full = jnp.tile(jnp.concatenate([-sin, sin], axis=-1), (1, reps))
    return cos_full, sin_full


def kernel(x, norm_mix_g, w_in, rw_mu, rw_w0, rw_w2, rw_a0, rw_a2, rw_g2, rw_k_k, rw_k_a,
           rw_r_k, rw_ln_g, rw_ln_b, da_lq1, da_lk1, da_lq2, da_lk2, da_subln_g,
           w_branch_a, w_branch_b, w_o, norm_ffn_g, w_ff1, w_ff2, norm_final_g):
    b, s, d = x.shape
    assert d == D_MODEL and norm_mix_g.shape[0] == 1
    assert s % ATTN_BLOCK == 0 and s % INPROJ_ROWS == 0 and s % RW_PREP_ROWS == 0
    assert (b * s) % FFN_ROWS == 0
    row = lambda t: t.reshape(1, -1)
    x2d = x.reshape(b * s, d)

    cos, sin = _rope_tables(s)
    scale = DA_HALF_DIM ** -0.5
    q, k, v, rw, gates = _inproj(x2d, norm_mix_g, w_in[0].astype(BF16),
                                 cos * scale, sin * scale, cos, sin, s)

    ya = _attention(q.reshape(b, s, DA_WIDTH), k.reshape(b, s, DA_WIDTH),
                    v.reshape(b, s, DA_WIDTH), row(da_lq1), row(da_lk1), row(da_lq2),
                    row(da_lk2), row(da_subln_g))

    wa2 = jnp.zeros((LANES, 2 * RW_WIDTH), F32)
    wa2 = wa2.at[:RW_DECAY_LORA, :RW_WIDTH].set(rw_w2[0])
    wa2 = wa2.at[RW_DECAY_LORA:, RW_WIDTH:].set(rw_a2[0])
    idx = jnp.arange(SEG_TILE) // RW_HEAD
    ones_bd = (idx[:, None] == idx[None, :]).astype(BF16)
    tok = jnp.arange(RW_PREP_ROWS)
    tri = ((tok[:, None] >= tok[None, :])
           & (tok[:, None] // CHUNK == tok[None, :] // CHUNK)).astype(BF16)

    rm, yg, bonus, g = _rwkv_prep(
        rw, rw_mu, rw_w0, rw_a0, rw_k_k, rw_k_a, row(rw_r_k),
        wa2.astype(BF16), rw_g2[0].astype(BF16), ones_bd, tri, s)
    nc = s // CHUNK
    yb = _rwkv_scan(rm.reshape(b, nc, 2 * CHUNK, RW_WIDTH), yg.reshape(b, nc, 2 * CHUNK, RW_WIDTH),
                    bonus.reshape(b, s, RW_WIDTH), g.reshape(b, s, RW_WIDTH),
                    rw_ln_g, rw_ln_b, ones_bd)

    out = _merge_ffn(x2d, ya.reshape(b * s, DA_WIDTH), yb.reshape(b * s, RW_WIDTH), gates,
                     w_branch_a[0].astype(BF16), w_branch_b[0].astype(BF16),
                     w_o[0].astype(BF16), norm_ffn_g, w_ff1[0].astype(BF16),
                     w_ff2[0].astype(BF16), row(norm_final_g))
    return out.reshape(b, s, d)
```

```python
import functools
import math

import jax
import jax.numpy as jnp
from jax import lax
from jax.experimental import pallas as pl
from jax.experimental.pallas import tpu as pltpu

F32 = jnp.float32
BF16 = jnp.bfloat16

D_MODEL = 1024
CHUNK = 64
NORM_EPS = 1e-6
ROPE_THETA = 10000.0
DA_WIDTH = 512
DA_HEADS = 4
DA_HALF_DIM = 64
DA_VDIM = 128
DA_SUBLN_EPS = 1e-5
RW_WIDTH = 512
RW_HEAD = 64
RW_PAIRS = RW_WIDTH // (2 * RW_HEAD)
RW_DECAY_LORA = 64
RW_AAA_LORA = 64
RW_GATE_LORA = 128
RW_GN_EPS = RW_HEAD * 1e-5
RW_IN_WIDTH = 3 * RW_WIDTH + RW_DECAY_LORA + RW_AAA_LORA + RW_GATE_LORA
DA_IN_WIDTH = 3 * DA_WIDTH
GATE_WIDTH = 2 * D_MODEL
D_IN = DA_IN_WIDTH + RW_IN_WIDTH + GATE_WIDTH
D_FF = 4 * D_MODEL
LAMBDA_INIT = 0.8 - 0.6 * math.exp(0.0)
Q_SCALE = DA_HALF_DIM ** -0.5 * math.log2(math.e)

LANES = 128
MXU_TILE = 256
SEG_TILE = MXU_TILE
VMEM_LIMIT_BYTES = 56 * 1024 * 1024

INPROJ_ROWS = 512
ATTN_BLOCK = 256
ATTN_HEAD_GROUP = 4
RW_PREP_ROWS = 256
FFN_ROWS = 512
FFN_CHUNK = 1024


def _dot(a, b):
    return jnp.dot(a, b, preferred_element_type=F32)


def _dot_nt(a, b):
    return lax.dot_general(a, b, (((1,), (1,)), ((), ())), preferred_element_type=F32)


def _split2(x):
    hi = x.astype(BF16)
    lo = (x - hi.astype(F32)).astype(BF16)
    return hi, lo


def _segsum(x, ones_bd):
    hi, lo = _split2(x)
    w = ones_bd.shape[0]
    parts = [_dot(hi[:, c:c + w], ones_bd) + _dot(lo[:, c:c + w], ones_bd)
             for c in range(0, x.shape[1], w)]
    return jnp.concatenate(parts, axis=1)


def _resident(shape):
    nd = len(shape)
    return pl.BlockSpec(shape, lambda *_: (0,) * nd, pipeline_mode=pl.Buffered(1))


def _inproj_kernel(x_ref, g_ref, w_ref, cos_ref, sin_ref,
                   q_ref, k_ref, v_ref, rw_ref, gate_ref):
    x = x_ref[...]
    ms = jnp.mean(x * x, axis=-1, keepdims=True)
    h = (x * lax.rsqrt(ms + NORM_EPS) * g_ref[...]).astype(BF16)

    def proj(c0, c1):
        return _dot(h, w_ref[:, c0:c1])

    rows = x.shape[0]
    lane = lax.broadcasted_iota(jnp.int32, (rows, DA_WIDTH), 1)
    first_half = (lane & (DA_HALF_DIM - 1)) < (DA_HALF_DIM // 2)

    reps = DA_WIDTH // LANES
    cos = jnp.concatenate([cos_ref[...]] * reps, axis=1)
    sin = jnp.concatenate([sin_ref[...]] * reps, axis=1)

    def rope(t):
        partner = jnp.where(first_half,
                            pltpu.roll(t, DA_WIDTH - DA_HALF_DIM // 2, 1),
                            pltpu.roll(t, DA_HALF_DIM // 2, 1))
        return t * cos + partner * sin

    q_ref[...] = (rope(proj(0, DA_WIDTH)) * Q_SCALE).astype(q_ref.dtype)
    k_ref[...] = rope(proj(DA_WIDTH, 2 * DA_WIDTH)).astype(k_ref.dtype)
    v_ref[...] = proj(2 * DA_WIDTH, 3 * DA_WIDTH).astype(v_ref.dtype)
    rw_ref[...] = proj(DA_IN_WIDTH, DA_IN_WIDTH + RW_IN_WIDTH)
    gate_ref[...] = proj(DA_IN_WIDTH + RW_IN_WIDTH, D_IN)


def _inproj(x2d, g, w_bf, cos, sin, seq):
    m = x2d.shape[0]
    tm = INPROJ_ROWS
    nseq = seq // tm
    row = lambda i: (i, 0)
    tab = lambda i: (i % nseq, 0)
    return pl.pallas_call(
        _inproj_kernel,
        grid=(m // tm,),
        in_specs=[
            pl.BlockSpec((tm, D_MODEL), row),
            _resident((1, D_MODEL)),
            _resident((D_MODEL, D_IN)),
            pl.BlockSpec((tm, LANES), tab),
            pl.BlockSpec((tm, LANES), tab),
        ],
        out_specs=[
            pl.BlockSpec((tm, DA_WIDTH), row),
            pl.BlockSpec((tm, DA_WIDTH), row),
            pl.BlockSpec((tm, DA_WIDTH), row),
            pl.BlockSpec((tm, RW_IN_WIDTH), row),
            pl.BlockSpec((tm, GATE_WIDTH), row),
        ],
        out_shape=[
            jax.ShapeDtypeStruct((m, DA_WIDTH), BF16),
            jax.ShapeDtypeStruct((m, DA_WIDTH), BF16),
            jax.ShapeDtypeStruct((m, DA_WIDTH), BF16),
            jax.ShapeDtypeStruct((m, RW_IN_WIDTH), F32),
            jax.ShapeDtypeStruct((m, GATE_WIDTH), F32),
        ],
        compiler_params=pltpu.CompilerParams(
            dimension_semantics=("parallel",), vmem_limit_bytes=VMEM_LIMIT_BYTES),
        name="inproj",
    )(x2d, g, w_bf, cos, sin)


def _attn_kernel(lq1_ref, lk1_ref, lq2_ref, lk2_ref, sg_ref, q_ref, k_ref, v_ref,
                 o_ref):
    tq = q_ref.shape[0]
    heads = q_ref.shape[1] // DA_VDIM
    i = pl.program_id(2)
    lane = lax.broadcasted_iota(jnp.int32, (tq, DA_VDIM), 1)
    lam = (jnp.exp(jnp.sum(lq1_ref[...] * lk1_ref[...], axis=-1, keepdims=True))
           - jnp.exp(jnp.sum(lq2_ref[...] * lk2_ref[...], axis=-1, keepdims=True))
           + LAMBDA_INIT)
    r = lax.broadcasted_iota(jnp.int32, (2 * tq, tq), 0)
    c = lax.broadcasted_iota(jnp.int32, (2 * tq, tq), 1)
    allowed = (c // CHUNK) <= ((r & (tq - 1)) // CHUNK)

    def attend(nfull, h):
        cols = slice(h * DA_VDIM, (h + 1) * DA_VDIM)
        q = q_ref[:, cols]
        zero = jnp.zeros_like(q)
        qq = jnp.concatenate([jnp.where(lane < DA_HALF_DIM, q, zero),
                              jnp.where(lane >= DA_HALF_DIM, q, zero)], axis=0)
        s = []
        for j in range(nfull + 1):
            s_j = _dot_nt(qq, k_ref[j * tq:(j + 1) * tq, cols])
            s.append(jnp.where(allowed, s_j, -jnp.inf) if j == nfull else s_j)
            yield
        m_lane = s[0]
        for s_j in s[1:]:
            m_lane = jnp.maximum(m_lane, s_j)
        m = jnp.max(m_lane, axis=-1, keepdims=True)
        l_lane = None
        acc = None
        for j, s_j in enumerate(s):
            p = jnp.exp2(s_j - m)
            pv = _dot(p.astype(BF16), v_ref[j * tq:(j + 1) * tq, cols])
            l_lane = p if l_lane is None else l_lane + p
            acc = pv if acc is None else acc + pv
            if j < nfull:
                yield
        l = jnp.sum(l_lane, axis=-1, keepdims=True)
        o = acc / l
        od = o[:tq] - lam * o[tq:]
        ms = jnp.mean(od * od, axis=-1, keepdims=True)
        y = od * lax.rsqrt(ms + DA_SUBLN_EPS) * sg_ref[...]
        o_ref[:, cols] = (y * (1.0 - LAMBDA_INIT)).astype(o_ref.dtype)

    def variant(nfull):
        gens = [attend(nfull, h) for h in range(heads)]
        nscore = nfull + 1
        for _ in range(nscore):
            next(gens[0])
        for h in range(heads):
            ahead = gens[h + 1] if h + 1 < heads else None
            left = nscore if ahead is not None else 0
            for _ in gens[h]:
                if left:
                    next(ahead)
                    left -= 1
            for _ in range(left):
                next(ahead)

    for n in range(k_ref.shape[0] // tq):
        pl.when(i == n)(functools.partial(variant, n))


def _attention(q, k, v, lq1, lk1, lq2, lk2, subln_g):
    b, s, _ = q.shape
    tq = ATTN_BLOCK
    assert tq & (tq - 1) == 0 and s % tq == 0
    vec = lambda n: pl.BlockSpec((1, n), lambda bi, hi, qi: (0, 0))
    gw = ATTN_HEAD_GROUP * DA_VDIM
    return pl.pallas_call(
        _attn_kernel,
        grid=(b, DA_HEADS // ATTN_HEAD_GROUP, s // tq),
        in_specs=[
            vec(DA_HALF_DIM), vec(DA_HALF_DIM), vec(DA_HALF_DIM), vec(DA_HALF_DIM),
            vec(DA_VDIM),
            pl.BlockSpec((None, tq, gw), lambda bi, hi, qi: (bi, qi, hi)),
            pl.BlockSpec((None, s, gw), lambda bi, hi, qi: (bi, 0, hi)),
            pl.BlockSpec((None, s, gw), lambda bi, hi, qi: (bi, 0, hi)),
        ],
        out_specs=pl.BlockSpec((None, tq, gw), lambda bi, hi, qi: (bi, qi, hi)),
        out_shape=jax.ShapeDtypeStruct((b, s, DA_WIDTH), BF16),
        compiler_params=pltpu.CompilerParams(
            dimension_semantics=("parallel", "parallel", "arbitrary"),
            vmem_limit_bytes=VMEM_LIMIT_BYTES),
        name="diff_attn",
    )(lq1, lk1, lq2, lk2, subln_g, q, k, v)


_STAGED = ("r_t", "a_t", "k_t", "b_t", "k_h", "b_h", "v", "gam_c")


def _rwkv_stage(seq_start, rw_ref, prev_ref, mu_ref, w0_ref, a0_ref, kk_ref, ka_ref, rk_ref,
                wa2_ref, g2_ref, ones_ref, tri_ref, bonus_ref, g_ref, stage_ref):
    tt = rw_ref.shape[0]
    z = rw_ref[...]
    prev = jnp.where(seq_start, 0.0, prev_ref[7:8, :])
    row = lax.broadcasted_iota(jnp.int32, z.shape, 0)
    z_prev = jnp.where(row == 0, prev, pltpu.roll(z, 1, 0))
    zs = z + (z_prev - z) * mu_ref[...]
    yield

    w3 = 3 * RW_WIDTH
    r = zs[:, 0:RW_WIDTH]
    k = zs[:, RW_WIDTH:2 * RW_WIDTH]
    v = zs[:, 2 * RW_WIDTH:w3]
    x_wa = zs[:, w3:w3 + LANES]
    x_g = zs[:, w3 + LANES:w3 + 2 * LANES]
    lane = lax.broadcasted_iota(jnp.int32, x_wa.shape, 1)
    t_wa = jnp.where(lane < RW_DECAY_LORA, jnp.tanh(x_wa), x_wa)
    lora = _dot(t_wa.astype(BF16), wa2_ref[...])
    u = -(w0_ref[...] + lora[:, :RW_WIDTH])
    softplus = jnp.maximum(u, 0.0) + jnp.log(1.0 + jnp.exp(-jnp.abs(u)))
    log_decay = -jnp.exp(-softplus - 0.5)
    yield
    a_lr = jax.nn.sigmoid(a0_ref[...] + lora[:, RW_WIDTH:])
    g_ref[...] = _dot(jax.nn.sigmoid(x_g).astype(BF16), g2_ref[...])
    yield

    ones_bd = ones_ref[...]
    kk = k * kk_ref[...]
    kk = kk * jnp.minimum(lax.rsqrt(_segsum(kk * kk, ones_bd)), 1e12)
    yield
    k2 = k * (1.0 + (a_lr - 1.0) * ka_ref[...])
    a_vec = -kk
    b_vec = kk * a_lr
    bonus_ref[...] = _segsum(r * k2 * rk_ref[...], ones_bd) * v
    yield

    tri = tri_ref[...]
    h1 = log_decay.astype(BF16)
    r1 = log_decay - h1.astype(F32)
    h2 = r1.astype(BF16)
    h3 = (r1 - h2.astype(F32)).astype(BF16)
    cs = _dot(tri, h1) + _dot(tri, h2) + _dot(tri, h3)
    nchunk = tt // CHUNK
    cl = jnp.concatenate(
        [jnp.broadcast_to(cs[(c + 1) * CHUNK - 1:(c + 1) * CHUNK, :], (CHUNK, RW_WIDTH))
         for c in range(nchunk)], axis=0)
    yield
    stage_ref[_STAGED.index("v")] = v
    stage_ref[_STAGED.index("gam_c")] = jnp.exp(cl)
    stage_ref[_STAGED.index("r_t")] = r * jnp.exp(cs)
    yield
    stage_ref[_STAGED.index("a_t")] = a_vec * jnp.exp(cs - log_decay)
    yield
    gam_inv = jnp.exp(-cs)
    stage_ref[_STAGED.index("k_t")] = k2 * gam_inv
    stage_ref[_STAGED.index("b_t")] = b_vec * gam_inv
    yield
    gam_end = jnp.exp(cl - cs)
    stage_ref[_STAGED.index("k_h")] = k2 * gam_end
    stage_ref[_STAGED.index("b_h")] = b_vec * gam_end


def _rwkv_chain(stage_ref, rm_ref, yg_ref):
    r_t, a_t, k_t, b_t, k_h, b_h, v, gam_c = _STAGED
    tt = stage_ref.shape[1]
    nchunk = tt // CHUNK

    quad = 2 * LANES
    cq = (CHUNK, quad)
    t_idx = lax.broadcasted_iota(jnp.int32, cq, 0)
    l_idx = lax.broadcasted_iota(jnp.int32, cq, 1)
    s_idx = l_idx & (RW_HEAD - 1)
    strict = s_idx < t_idx
    incl = s_idx <= t_idx
    diag = s_idx == t_idx
    even_head = (l_idx & RW_HEAD) == 0
    r4 = lax.broadcasted_iota(jnp.int32, (quad, quad), 0)
    l4 = lax.broadcasted_iota(jnp.int32, (quad, quad), 1)
    bd4_mask = (r4 // RW_HEAD) == (l4 // RW_HEAD)
    r2 = lax.broadcasted_iota(jnp.int32, (2 * CHUNK, LANES), 0)
    l2 = lax.broadcasted_iota(jnp.int32, (2 * CHUNK, LANES), 1)
    bd_mask = (r2 < CHUNK) == (l2 < RW_HEAD)
    eye = (r2 == l2).astype(F32)
    eye2 = jnp.concatenate([eye, eye], axis=0)
    z_pair = jnp.zeros((2 * CHUNK, LANES), F32)
    z_half = jnp.zeros((CHUNK, LANES), F32)

    def bd(x):
        return jnp.where(bd_mask, jnp.concatenate([x, x], axis=0), 0.0)

    def bd4(x):
        return jnp.where(bd4_mask, jnp.concatenate([x, x, x, x], axis=0), 0.0)

    def blockdiag2(xa, xb):
        return jnp.concatenate([jnp.concatenate([xa, z_pair], axis=1),
                                jnp.concatenate([z_pair, xb], axis=1)], axis=0)

    def stacked_bd(st):
        return blockdiag2(st[:2 * CHUNK], st[2 * CHUNK:])

    items = [(c, qd) for c in range(nchunk) for qd in range(RW_WIDTH // quad)]

    def view(name, item):
        c, qd = item
        return stage_ref[_STAGED.index(name), c * CHUNK:(c + 1) * CHUNK, qd * quad:(qd + 1) * quad]

    a2 = [_dot_nt(jnp.concatenate([view(a_t, it), view(r_t, it)], axis=0).astype(BF16),
                  jnp.concatenate([bd4(view(b_t, it)), bd4(view(k_t, it))], axis=0).astype(BF16))
          for it in items]
    yield
    a_ab = [jnp.where(strict, z[:CHUNK, :quad], 0.0) for z in a2]
    a_ak = [jnp.where(strict, z[:CHUNK, quad:], 0.0) for z in a2]
    a_rb = [jnp.where(incl, z[CHUNK:, :quad], 0.0) for z in a2]
    a_rk = [jnp.where(incl, z[CHUNK:, quad:], 0.0) for z in a2]

    n_st = [jnp.concatenate([bd(z[:, :LANES]), bd(z[:, LANES:])], axis=0) for z in a_ab]
    t_st = [eye2 + n for n in n_st]
    p_st = [_dot(stacked_bd(n).astype(BF16), n.astype(BF16)) for n in n_st]
    yield
    for _ in range(4):
        zz = [_dot(stacked_bd(p).astype(BF16), jnp.concatenate([p, t], axis=1).astype(BF16))
              for p, t in zip(p_st, t_st)]
        p_st = [z[:, :LANES] for z in zz]
        t_st = [t + z[:, LANES:] for t, z in zip(t_st, zz)]
        yield
    t_st = [t + _dot(stacked_bd(p).astype(BF16), t.astype(BF16)) for p, t in zip(p_st, t_st)]
    yield

    bd4_v = [bd4(view(v, it)) for it in items]
    av = [_dot(x.astype(BF16), bv.astype(BF16)) for x, bv in zip(a_ak, bd4_v)]
    x_st = [jnp.concatenate(
        [jnp.concatenate([bd(view(a_t, it)[:, :LANES]), bd(u[:, :LANES])], axis=1),
         jnp.concatenate([bd(view(a_t, it)[:, LANES:]), bd(u[:, LANES:])], axis=1)], axis=0)
        for it, u in zip(items, av)]
    tx = [_dot(stacked_bd(t).astype(BF16), x.astype(BF16)) for t, x in zip(t_st, x_st)]
    yield

    for n_item, (it, tx_i, arb, ark, bv4) in enumerate(zip(items, tx, a_rb, a_rk, bd4_v)):
        if n_item and n_item % 2 == 0:
            yield
        c, qd = it
        r_bar, y_bar, au = [], [], []
        for half in range(2):
            pl_ = slice(half * LANES, (half + 1) * LANES)
            rows = slice(half * 2 * CHUNK, (half + 1) * 2 * CHUNK)
            tx_p = tx_i[rows]
            rhs = jnp.concatenate(
                [tx_p, jnp.concatenate([z_pair, bv4[rows, pl_]], axis=1)], axis=0)
            ry = _dot(jnp.concatenate([arb[:, pl_], ark[:, pl_]], axis=1).astype(BF16),
                      rhs.astype(BF16))
            r_bar.append(view(r_t, it)[:, pl_] + ry[:, :LANES])
            y_bar.append(ry[:, LANES:])
            au.append(tx_p[:CHUNK] + tx_p[CHUNK:])
        v_q = view(v, it)
        mg_rhs = jnp.concatenate(
            [au[0], jnp.concatenate([z_half, v_q[:, :LANES]], axis=1),
             au[1], jnp.concatenate([z_half, v_q[:, LANES:]], axis=1)], axis=0)
        lhs_t = jnp.concatenate([view(b_h, it), view(k_h, it)], axis=0).T
        mg = _dot(stacked_bd(lhs_t).astype(BF16), mg_rhs.astype(BF16))
        m_sel = jnp.concatenate([mg[0:CHUNK], mg[2 * CHUNK:3 * CHUNK]], axis=1)
        m_odd = jnp.concatenate([mg[CHUNK:2 * CHUNK], mg[3 * CHUNK:]], axis=1)
        m_even_odd = jnp.where(jnp.concatenate([even_head, even_head], axis=1), m_sel, m_odd)
        m_quad = (jnp.concatenate([m_even_odd[:, :LANES], m_even_odd[:, 2 * LANES:3 * LANES]], axis=1)
                  + jnp.where(diag, view(gam_c, it), 0.0))
        g_quad = jnp.concatenate([m_even_odd[:, LANES:2 * LANES], m_even_odd[:, 3 * LANES:]], axis=1)
        cols = slice(qd * quad, (qd + 1) * quad)
        rm_ref[c, :, cols] = jnp.concatenate(
            [jnp.concatenate(r_bar, axis=1), m_quad], axis=0).astype(rm_ref.dtype)
        yg_ref[c, :, cols] = jnp.concatenate([jnp.concatenate(y_bar, axis=1), g_quad], axis=0)


def _rwkv_prep_kernel(rw_ref, prev_ref, mu_ref, w0_ref, a0_ref, kk_ref, ka_ref, rk_ref,
                      wa2_ref, g2_ref, ones_ref, tri_ref,
                      rm_ref, yg_ref, bonus_ref, g_ref, stage_a, stage_b,
                      *, ntiles, tiles_per_seq):
    t = pl.program_id(0)
    tile = jnp.minimum(t, ntiles - 1)
    seq_start = (tile % tiles_per_seq) == 0

    @pl.when(t == 0)
    def _():
        stage_b[...] = jnp.zeros(stage_b.shape, F32)

    def step(write_ref, read_ref):
        pending = [
            _rwkv_chain(read_ref, rm_ref, yg_ref),
            _rwkv_stage(seq_start, rw_ref, prev_ref, mu_ref, w0_ref, a0_ref, kk_ref, ka_ref,
                        rk_ref, wa2_ref, g2_ref, ones_ref, tri_ref, bonus_ref, g_ref, write_ref),
        ]
        while pending:
            for gen in list(pending):
                try:
                    next(gen)
                except StopIteration:
                    pending.remove(gen)

    pl.when(t % 2 == 0)(functools.partial(step, stage_a, stage_b))
    pl.when(t % 2 == 1)(functools.partial(step, stage_b, stage_a))


def _rwkv_prep(rw, mu, w0, a0, k_k, k_a, r_k, wa2, g2, ones_bd, tri, seq):
    m = rw.shape[0]
    tt = RW_PREP_ROWS
    cpt = tt // CHUNK
    ntiles = m // tt
    cur = lambda t: jnp.minimum(t, ntiles - 1)
    done = lambda t: jnp.maximum(t - 1, 0)
    vec = lambda n: pl.BlockSpec((1, n), lambda t: (0, 0))
    return pl.pallas_call(
        functools.partial(_rwkv_prep_kernel, ntiles=ntiles, tiles_per_seq=seq // tt),
        grid=(ntiles + 1,),
        in_specs=[
            pl.BlockSpec((tt, RW_IN_WIDTH), lambda t: (cur(t), 0)),
            pl.BlockSpec((8, RW_IN_WIDTH), lambda t: (jnp.maximum(cur(t) * (tt // 8) - 1, 0), 0)),
            vec(RW_IN_WIDTH), vec(RW_WIDTH), vec(RW_WIDTH), vec(RW_WIDTH), vec(RW_WIDTH),
            vec(RW_WIDTH),
            pl.BlockSpec((LANES, 2 * RW_WIDTH), lambda t: (0, 0)),
            pl.BlockSpec((RW_GATE_LORA, RW_WIDTH), lambda t: (0, 0)),
            pl.BlockSpec((SEG_TILE, SEG_TILE), lambda t: (0, 0)),
            pl.BlockSpec((tt, tt), lambda t: (0, 0)),
        ],
        out_specs=[
            pl.BlockSpec((cpt, 2 * CHUNK, RW_WIDTH), lambda t: (done(t), 0, 0)),
            pl.BlockSpec((cpt, 2 * CHUNK, RW_WIDTH), lambda t: (done(t), 0, 0)),
            pl.BlockSpec((tt, RW_WIDTH), lambda t: (cur(t), 0)),
            pl.BlockSpec((tt, RW_WIDTH), lambda t: (cur(t), 0)),
        ],
        out_shape=[
            jax.ShapeDtypeStruct((m // CHUNK, 2 * CHUNK, RW_WIDTH), BF16),
            jax.ShapeDtypeStruct((m // CHUNK, 2 * CHUNK, RW_WIDTH), F32),
            jax.ShapeDtypeStruct((m, RW_WIDTH), F32),
            jax.ShapeDtypeStruct((m, RW_WIDTH), F32),
        ],
        scratch_shapes=[
            pltpu.VMEM((len(_STAGED), tt, RW_WIDTH), F32),
            pltpu.VMEM((len(_STAGED), tt, RW_WIDTH), F32),
        ],
        compiler_params=pltpu.CompilerParams(
            dimension_semantics=("arbitrary",), vmem_limit_bytes=VMEM_LIMIT_BYTES),
        name="rwkv_prep",
    )(rw, rw, mu, w0, a0, k_k, k_a, r_k, wa2, g2, ones_bd, tri)


def _rwkv_scan_kernel(rm_ref, yg_ref, bonus_ref, g_ref, lng_ref, lnb_ref, ones_ref,
                      y_ref, h_sc, y_sc):
    c = pl.program_id(0)
    nb = rm_ref.shape[0]

    @pl.when(c == 0)
    def _():
        h_sc[...] = jnp.zeros(h_sc.shape, F32)

    r128 = lax.broadcasted_iota(jnp.int32, (2 * CHUNK, LANES), 0)
    l128 = lax.broadcasted_iota(jnp.int32, (2 * CHUNK, LANES), 1)
    bd_mask = (r128 < CHUNK) == (l128 < RW_HEAD)

    for bi in range(nb):
        for p in range(RW_PAIRS):
            lanes = slice(p * LANES, (p + 1) * LANES)
            h_bd = h_sc[bi * RW_PAIRS + p]
            out = _dot(rm_ref[bi, :, lanes], h_bd.astype(BF16)) + yg_ref[bi, :, lanes]
            y_sc[bi, :, lanes] = out[:CHUNK]
            h_new = out[CHUNK:]
            h_sc[bi * RW_PAIRS + p] = jnp.where(
                bd_mask, jnp.concatenate([h_new, h_new], axis=0), 0.0)

    ones_bd = ones_ref[...]
    y = y_sc[...].reshape(nb * CHUNK, RW_WIDTH)
    mean = _segsum(y, ones_bd) * (1.0 / RW_HEAD)
    d = y - mean
    var = _segsum(d * d, ones_bd) * (1.0 / RW_HEAD)
    yn = d * lax.rsqrt(var + RW_GN_EPS) * lng_ref[...] + lnb_ref[...]
    bonus = bonus_ref[...].reshape(nb * CHUNK, RW_WIDTH)
    gate = g_ref[...].reshape(nb * CHUNK, RW_WIDTH)
    y_ref[...] = ((yn + bonus) * gate).reshape(nb, CHUNK, RW_WIDTH).astype(y_ref.dtype)


def _rwkv_scan(rm, yg, bonus, g, ln_g, ln_b, ones_bd):
    b, nc, _, _ = rm.shape
    s = nc * CHUNK
    vec = lambda n: pl.BlockSpec((1, n), lambda ci: (0, 0))
    return pl.pallas_call(
        _rwkv_scan_kernel,
        grid=(nc,),
        in_specs=[
            pl.BlockSpec((b, None, 2 * CHUNK, RW_WIDTH), lambda ci: (0, ci, 0, 0)),
            pl.BlockSpec((b, None, 2 * CHUNK, RW_WIDTH), lambda ci: (0, ci, 0, 0)),
            pl.BlockSpec((b, CHUNK, RW_WIDTH), lambda ci: (0, ci, 0)),
            pl.BlockSpec((b, CHUNK, RW_WIDTH), lambda ci: (0, ci, 0)),
            vec(RW_WIDTH), vec(RW_WIDTH),
            pl.BlockSpec((SEG_TILE, SEG_TILE), lambda ci: (0, 0)),
        ],
        out_specs=pl.BlockSpec((b, CHUNK, RW_WIDTH), lambda ci: (0, ci, 0)),
        out_shape=jax.ShapeDtypeStruct((b, s, RW_WIDTH), BF16),
        scratch_shapes=[
            pltpu.VMEM((b * RW_PAIRS, 2 * CHUNK, LANES), F32),
            pltpu.VMEM((b, CHUNK, RW_WIDTH), F32),
        ],
        compiler_params=pltpu.CompilerParams(
            dimension_semantics=("arbitrary",), vmem_limit_bytes=VMEM_LIMIT_BYTES),
        name="rwkv_scan",
    )(rm, yg, bonus, g, ln_g, ln_b, ones_bd)


def _merge_ffn_kernel(x_ref, ya_ref, yb_ref, gate_ref, wa_ref, wb_ref, wo_ref,
                      gffn_ref, w1_ref, w2_ref, gfin_ref, o_ref):
    pa = _dot(ya_ref[...], wa_ref[...])
    pb = _dot(yb_ref[...], wb_ref[...])
    gate = gate_ref[...]
    merged = (jax.nn.sigmoid(gate[:, :D_MODEL]) * pa
              + jax.nn.sigmoid(gate[:, D_MODEL:]) * pb)
    x1 = x_ref[...] + _dot(merged.astype(BF16), wo_ref[...])
    ms = jnp.mean(x1 * x1, axis=-1, keepdims=True)
    h = (x1 * lax.rsqrt(ms + NORM_EPS) * gffn_ref[...]).astype(BF16)
    acc = x1
    for c in range(D_FF // FFN_CHUNK):
        cols = slice(c * FFN_CHUNK, (c + 1) * FFN_CHUNK)
        f = jnp.maximum(_dot(h, w1_ref[:, cols]), 0.0)
        acc = acc + _dot((f * f).astype(BF16), w2_ref[cols, :])
    ms2 = jnp.mean(acc * acc, axis=-1, keepdims=True)
    o_ref[...] = acc * lax.rsqrt(ms2 + NORM_EPS) * gfin_ref[...]


def _merge_ffn(x2d, ya, yb, gates, wa, wb, wo, gffn, w1, w2, gfin):
    m = x2d.shape[0]
    tm = FFN_ROWS
    row = lambda i: (i, 0)
    return pl.pallas_call(
        _merge_ffn_kernel,
        grid=(m // tm,),
        in_specs=[
            pl.BlockSpec((tm, D_MODEL), row),
            pl.BlockSpec((tm, DA_WIDTH), row),
            pl.BlockSpec((tm, RW_WIDTH), row),
            pl.BlockSpec((tm, GATE_WIDTH), row),
            _resident((DA_WIDTH, D_MODEL)),
            _resident((RW_WIDTH, D_MODEL)),
            _resident((D_MODEL, D_MODEL)),
            _resident((1, D_MODEL)),
            _resident((D_MODEL, D_FF)),
            _resident((D_FF, D_MODEL)),
            _resident((1, D_MODEL)),
        ],
        out_specs=pl.BlockSpec((tm, D_MODEL), row),
        out_shape=jax.ShapeDtypeStruct((m, D_MODEL), F32),
        compiler_params=pltpu.CompilerParams(
            dimension_semantics=("parallel",), vmem_limit_bytes=VMEM_LIMIT_BYTES),
        name="merge_ffn",
    )(x2d, ya, yb, gates, wa, wb, wo, gffn, w1, w2, gfin)


def _rope_tables(seq):
    d = DA_HALF_DIM
    pos = jnp.arange(seq, dtype=F32)
    inv_freq = ROPE_THETA ** (-jnp.arange(0, d, 2, dtype=F32) / d)
    ang = pos[:, None] * inv_freq[None, :]
    cos = jnp.cos(ang)
    sin = jnp.sin(ang)
    reps = LANES // d
    cos_full = jnp.tile(jnp.concatenate([cos, cos], axis=-1), (1, reps))
    sin_full = jnp.tile(jnp.concatenate([-sin, sin], axis=-1), (1, reps))
    return cos_full, sin_full


def kernel(x, norm_mix_g, w_in, rw_mu, rw_w0, rw_w2, rw_a0, rw_a2, rw_g2, rw_k_k, rw_k_a,
           rw_r_k, rw_ln_g, rw_ln_b, da_lq1, da_lk1, da_lq2, da_lk2, da_subln_g,
           w_branch_a, w_branch_b, w_o, norm_ffn_g, w_ff1, w_ff2, norm_final_g):
    b, s, d = x.shape
    assert d == D_MODEL and norm_mix_g.shape[0] == 1
    assert s % ATTN_BLOCK == 0 and s % INPROJ_ROWS == 0 and s % RW_PREP_ROWS == 0
    assert (b * s) % FFN_ROWS == 0
    row = lambda t: t.reshape(1, -1)
    x2d = x.reshape(b * s, d)

    cos, sin = _rope_tables(s)
    q, k, v, rw, gates = _inproj(x2d, norm_mix_g, w_in[0].astype(BF16), cos, sin, s)

    ya = _attention(q.reshape(b, s, DA_WIDTH), k.reshape(b, s, DA_WIDTH),
                    v.reshape(b, s, DA_WIDTH), row(da_lq1), row(da_lk1), row(da_lq2),
                    row(da_lk2), row(da_subln_g))

    wa2 = jnp.zeros((LANES, 2 * RW_WIDTH), F32)
    wa2 = wa2.at[:RW_DECAY_LORA, :RW_WIDTH].set(rw_w2[0])
    wa2 = wa2.at[RW_DECAY_LORA:, RW_WIDTH:].set(rw_a2[0])
    idx = jnp.arange(SEG_TILE) // RW_HEAD
    ones_bd = (idx[:, None] == idx[None, :]).astype(BF16)
    tok = jnp.arange(RW_PREP_ROWS)
    tri = ((tok[:, None] >= tok[None, :])
           & (tok[:, None] // CHUNK == tok[None, :] // CHUNK)).astype(BF16)

    rm, yg, bonus, g = _rwkv_prep(
        rw, rw_mu, rw_w0, rw_a0, rw_k_k, rw_k_a, row(rw_r_k),
        wa2.astype(BF16), rw_g2[0].astype(BF16), ones_bd, tri, s)
    nc = s // CHUNK
    yb = _rwkv_scan(rm.reshape(b, nc, 2 * CHUNK, RW_WIDTH), yg.reshape(b, nc, 2 * CHUNK, RW_WIDTH),
                    bonus.reshape(b, s, RW_WIDTH), g.reshape(b, s, RW_WIDTH),
                    rw_ln_g, rw_ln_b, ones_bd)

    out = _merge_ffn(x2d, ya.reshape(b * s, DA_WIDTH), yb.reshape(b * s, RW_WIDTH), gates,
                     w_branch_a[0].astype(BF16), w_branch_b[0].astype(BF16),
                     w_o[0].astype(BF16), norm_ffn_g, w_ff1[0].astype(BF16),
                     w_ff2[0].astype(BF16), row(norm_final_g))
    return out.reshape(b, s, d)
```

```python
import functools
import math

import jax
import jax.numpy as jnp
from jax import lax
from jax.experimental import pallas as pl
from jax.experimental.pallas import tpu as pltpu

F32 = jnp.float32
BF16 = jnp.bfloat16

D_MODEL = 1024
CHUNK = 64
NORM_EPS = 1e-6
ROPE_THETA = 10000.0
DA_WIDTH = 512
DA_HEADS = 4
DA_HALF_DIM = 64
DA_VDIM = 128
DA_SUBLN_EPS = 1e-5
RW_WIDTH = 512
RW_HEAD = 64
RW_PAIRS = RW_WIDTH // (2 * RW_HEAD)
RW_DECAY_LORA = 64
RW_AAA_LORA = 64
RW_GATE_LORA = 128
RW_GN_EPS = RW_HEAD * 1e-5
RW_IN_WIDTH = 3 * RW_WIDTH + RW_DECAY_LORA + RW_AAA_LORA + RW_GATE_LORA
DA_IN_WIDTH = 3 * DA_WIDTH
GATE_WIDTH = 2 * D_MODEL
D_IN = DA_IN_WIDTH + RW_IN_WIDTH + GATE_WIDTH
D_FF = 4 * D_MODEL
LAMBDA_INIT = 0.8 - 0.6 * math.exp(0.0)
Q_SCALE = DA_HALF_DIM ** -0.5 * math.log2(math.e)

LANES = 128
MXU_TILE = 256
SEG_TILE = MXU_TILE
VMEM_LIMIT_BYTES = 56 * 1024 * 1024

INPROJ_ROWS = 512
ATTN_BLOCK = 256
ATTN_BATCH_GROUP = 2
RW_PREP_ROWS = 256
FFN_ROWS = 512
FFN_CHUNK = 1024


def _dot(a, b):
    return jnp.dot(a, b, preferred_element_type=F32)


def _dot_nt(a, b):
    return lax.dot_general(a, b, (((1,), (1,)), ((), ())), preferred_element_type=F32)


def _split2(x):
    hi = x.astype(BF16)
    lo = (x - hi.astype(F32)).astype(BF16)
    return hi, lo


def _segsum(x, ones_bd):
    hi, lo = _split2(x)
    w = ones_bd.shape[0]
    parts = [_dot(hi[:, c:c + w], ones_bd) + _dot(lo[:, c:c + w], ones_bd)
             for c in range(0, x.shape[1], w)]
    return jnp.concatenate(parts, axis=1)


def _resident(shape):
    nd = len(shape)
    return pl.BlockSpec(shape, lambda *_: (0,) * nd, pipeline_mode=pl.Buffered(1))


def _inproj_kernel(x_ref, g_ref, w_ref, cos_ref, sin_ref,
                   q_ref, k_ref, v_ref, rw_ref, gate_ref):
    x = x_ref[...]
    ms = jnp.mean(x * x, axis=-1, keepdims=True)
    h = (x * lax.rsqrt(ms + NORM_EPS) * g_ref[...]).astype(BF16)

    def proj(c0, c1):
        return _dot(h, w_ref[:, c0:c1])

    rows = x.shape[0]
    lane = lax.broadcasted_iota(jnp.int32, (rows, DA_WIDTH), 1)
    first_half = (lane & (DA_HALF_DIM - 1)) < (DA_HALF_DIM // 2)

    reps = DA_WIDTH // LANES
    cos = jnp.concatenate([cos_ref[...]] * reps, axis=1)
    sin = jnp.concatenate([sin_ref[...]] * reps, axis=1)

    def rope(t):
        partner = jnp.where(first_half,
                            pltpu.roll(t, DA_WIDTH - DA_HALF_DIM // 2, 1),
                            pltpu.roll(t, DA_HALF_DIM // 2, 1))
        return t * cos + partner * sin

    q_ref[...] = (rope(proj(0, DA_WIDTH)) * Q_SCALE).astype(q_ref.dtype)
    k_ref[...] = rope(proj(DA_WIDTH, 2 * DA_WIDTH)).astype(k_ref.dtype)
    v_ref[...] = proj(2 * DA_WIDTH, 3 * DA_WIDTH).astype(v_ref.dtype)
    rw_ref[...] = proj(DA_IN_WIDTH, DA_IN_WIDTH + RW_IN_WIDTH)
    gate_ref[...] = proj(DA_IN_WIDTH + RW_IN_WIDTH, D_IN)


def _inproj(x2d, g, w_bf, cos, sin, seq):
    m = x2d.shape[0]
    tm = INPROJ_ROWS
    nseq = seq // tm
    row = lambda i: (i, 0)
    tab = lambda i: (i % nseq, 0)
    return pl.pallas_call(
        _inproj_kernel,
        grid=(m // tm,),
        in_specs=[
            pl.BlockSpec((tm, D_MODEL), row),
            _resident((1, D_MODEL)),
            _resident((D_MODEL, D_IN)),
            pl.BlockSpec((tm, LANES), tab),
            pl.BlockSpec((tm, LANES), tab),
        ],
        out_specs=[
            pl.BlockSpec((tm, DA_WIDTH), row),
            pl.BlockSpec((tm, DA_WIDTH), row),
            pl.BlockSpec((tm, DA_WIDTH), row),
            pl.BlockSpec((tm, RW_IN_WIDTH), row),
            pl.BlockSpec((tm, GATE_WIDTH), row),
        ],
        out_shape=[
            jax.ShapeDtypeStruct((m, DA_WIDTH), BF16),
            jax.ShapeDtypeStruct((m, DA_WIDTH), BF16),
            jax.ShapeDtypeStruct((m, DA_WIDTH), BF16),
            jax.ShapeDtypeStruct((m, RW_IN_WIDTH), F32),
            jax.ShapeDtypeStruct((m, GATE_WIDTH), F32),
        ],
        compiler_params=pltpu.CompilerParams(
            dimension_semantics=("parallel",), vmem_limit_bytes=VMEM_LIMIT_BYTES),
        name="inproj",
    )(x2d, g, w_bf, cos, sin)


def _attn_kernel(lq1_ref, lk1_ref, lq2_ref, lk2_ref, sg_ref, q_ref, k_ref, v_ref,
                 o_ref):
    nbatch, tq, width = q_ref.shape
    heads = width // DA_VDIM
    i = pl.program_id(1)
    lane = lax.broadcasted_iota(jnp.int32, (tq, DA_VDIM), 1)
    lam = (jnp.exp(jnp.sum(lq1_ref[...] * lk1_ref[...], axis=-1, keepdims=True))
           - jnp.exp(jnp.sum(lq2_ref[...] * lk2_ref[...], axis=-1, keepdims=True))
           + LAMBDA_INIT)
    r = lax.broadcasted_iota(jnp.int32, (2 * tq, tq), 0)
    c = lax.broadcasted_iota(jnp.int32, (2 * tq, tq), 1)
    allowed = (c // CHUNK) <= ((r & (tq - 1)) // CHUNK)

    def attend(nfull, bi, h):
        cols = slice(h * DA_VDIM, (h + 1) * DA_VDIM)
        q = q_ref[bi, :, cols]
        zero = jnp.zeros_like(q)
        qq = jnp.concatenate([jnp.where(lane < DA_HALF_DIM, q, zero),
                              jnp.where(lane >= DA_HALF_DIM, q, zero)], axis=0)
        s = []
        for j in range(nfull + 1):
            s_j = _dot_nt(qq, k_ref[bi, j * tq:(j + 1) * tq, cols])
            s.append(jnp.where(allowed, s_j, -jnp.inf) if j == nfull else s_j)
            yield
        m_lane = s[0]
        for s_j in s[1:]:
            m_lane = jnp.maximum(m_lane, s_j)
        m = jnp.max(m_lane, axis=-1, keepdims=True)
        l_lane = None
        acc = None
        for j, s_j in enumerate(s):
            p = jnp.exp2(s_j - m)
            pv = _dot(p.astype(BF16), v_ref[bi, j * tq:(j + 1) * tq, cols])
            l_lane = p if l_lane is None else l_lane + p
            acc = pv if acc is None else acc + pv
            if j < nfull:
                yield
        l = jnp.sum(l_lane, axis=-1, keepdims=True)
        o = acc / l
        od = o[:tq] - lam * o[tq:]
        ms = jnp.mean(od * od, axis=-1, keepdims=True)
        y = od * lax.rsqrt(ms + DA_SUBLN_EPS) * sg_ref[...]
        o_ref[bi, :, cols] = (y * (1.0 - LAMBDA_INIT)).astype(o_ref.dtype)

    def variant(nfull):
        gens = [attend(nfull, bi, h) for bi in range(nbatch) for h in range(heads)]
        nscore = nfull + 1
        for _ in range(nscore):
            next(gens[0])
        for k, cur in enumerate(gens):
            ahead = gens[k + 1] if k + 1 < len(gens) else None
            left = nscore if ahead is not None else 0
            for _ in cur:
                if left:
                    next(ahead)
                    left -= 1
            for _ in range(left):
                next(ahead)

    for n in range(k_ref.shape[1] // tq):
        pl.when(i == n)(functools.partial(variant, n))


def _attention(q, k, v, lq1, lk1, lq2, lk2, subln_g):
    b, s, _ = q.shape
    tq = ATTN_BLOCK
    assert tq & (tq - 1) == 0 and s % tq == 0
    vec = lambda n: pl.BlockSpec((1, n), lambda bi, qi: (0, 0))
    nb = ATTN_BATCH_GROUP
    assert b % nb == 0
    return pl.pallas_call(
        _attn_kernel,
        grid=(b // nb, s // tq),
        in_specs=[
            vec(DA_HALF_DIM), vec(DA_HALF_DIM), vec(DA_HALF_DIM), vec(DA_HALF_DIM),
            vec(DA_VDIM),
            pl.BlockSpec((nb, tq, DA_WIDTH), lambda bi, qi: (bi, qi, 0)),
            pl.BlockSpec((nb, s, DA_WIDTH), lambda bi, qi: (bi, 0, 0)),
            pl.BlockSpec((nb, s, DA_WIDTH), lambda bi, qi: (bi, 0, 0)),
        ],
        out_specs=pl.BlockSpec((nb, tq, DA_WIDTH), lambda bi, qi: (bi, qi, 0)),
        out_shape=jax.ShapeDtypeStruct((b, s, DA_WIDTH), BF16),
        compiler_params=pltpu.CompilerParams(
            dimension_semantics=("parallel", "arbitrary"),
            vmem_limit_bytes=VMEM_LIMIT_BYTES),
        name="diff_attn",
    )(lq1, lk1, lq2, lk2, subln_g, q, k, v)


_STAGED = ("r_t", "a_t", "k_t", "b_t", "k_h", "b_h", "v", "gam_c")


def _rwkv_stage(seq_start, rw_ref, prev_ref, mu_ref, w0_ref, a0_ref, kk_ref, ka_ref, rk_ref,
                wa2_ref, g2_ref, ones_ref, tri_ref, bonus_ref, g_ref, stage_ref):
    tt = rw_ref.shape[0]
    z = rw_ref[...]
    prev = jnp.where(seq_start, 0.0, prev_ref[7:8, :])
    row = lax.broadcasted_iota(jnp.int32, z.shape, 0)
    z_prev = jnp.where(row == 0, prev, pltpu.roll(z, 1, 0))
    zs = z + (z_prev - z) * mu_ref[...]
    yield

    w3 = 3 * RW_WIDTH
    r = zs[:, 0:RW_WIDTH]
    k = zs[:, RW_WIDTH:2 * RW_WIDTH]
    v = zs[:, 2 * RW_WIDTH:w3]
    x_wa = zs[:, w3:w3 + LANES]
    x_g = zs[:, w3 + LANES:w3 + 2 * LANES]
    lane = lax.broadcasted_iota(jnp.int32, x_wa.shape, 1)
    t_wa = jnp.where(lane < RW_DECAY_LORA, jnp.tanh(x_wa), x_wa)
    lora = _dot(t_wa.astype(BF16), wa2_ref[...])
    u = -(w0_ref[...] + lora[:, :RW_WIDTH])
    softplus = jnp.maximum(u, 0.0) + jnp.log(1.0 + jnp.exp(-jnp.abs(u)))
    log_decay = -jnp.exp(-softplus - 0.5)
    yield
    a_lr = jax.nn.sigmoid(a0_ref[...] + lora[:, RW_WIDTH:])
    g_ref[...] = _dot(jax.nn.sigmoid(x_g).astype(BF16), g2_ref[...])
    yield

    ones_bd = ones_ref[...]
    kk = k * kk_ref[...]
    kk = kk * jnp.minimum(lax.rsqrt(_segsum(kk * kk, ones_bd)), 1e12)
    yield
    k2 = k * (1.0 + (a_lr - 1.0) * ka_ref[...])
    a_vec = -kk
    b_vec = kk * a_lr
    bonus_ref[...] = _segsum(r * k2 * rk_ref[...], ones_bd) * v
    yield

    tri = tri_ref[...]
    h1 = log_decay.astype(BF16)
    r1 = log_decay - h1.astype(F32)
    h2 = r1.astype(BF16)
    h3 = (r1 - h2.astype(F32)).astype(BF16)
    cs = _dot(tri, h1) + _dot(tri, h2) + _dot(tri, h3)
    nchunk = tt // CHUNK
    cl = jnp.concatenate(
        [jnp.broadcast_to(cs[(c + 1) * CHUNK - 1:(c + 1) * CHUNK, :], (CHUNK, RW_WIDTH))
         for c in range(nchunk)], axis=0)
    yield
    stage_ref[_STAGED.index("v")] = v
    stage_ref[_STAGED.index("gam_c")] = jnp.exp(cl)
    stage_ref[_STAGED.index("r_t")] = r * jnp.exp(cs)
    yield
    stage_ref[_STAGED.index("a_t")] = a_vec * jnp.exp(cs - log_decay)
    yield
    gam_inv = jnp.exp(-cs)
    stage_ref[_STAGED.index("k_t")] = k2 * gam_inv
    stage_ref[_STAGED.index("b_t")] = b_vec * gam_inv
    yield
    gam_end = jnp.exp(cl - cs)
    stage_ref[_STAGED.index("k_h")] = k2 * gam_end
    stage_ref[_STAGED.index("b_h")] = b_vec * gam_end


def _rwkv_chain(stage_ref, rm_ref, yg_ref):
    r_t, a_t, k_t, b_t, k_h, b_h, v, gam_c = _STAGED
    tt = stage_ref.shape[1]
    nchunk = tt // CHUNK

    quad = 2 * LANES
    cq = (CHUNK, quad)
    t_idx = lax.broadcasted_iota(jnp.int32, cq, 0)
    l_idx = lax.broadcasted_iota(jnp.int32, cq, 1)
    s_idx = l_idx & (RW_HEAD - 1)
    strict = s_idx < t_idx
    incl = s_idx <= t_idx
    diag = s_idx == t_idx
    even_head = (l_idx & RW_HEAD) == 0
    r4 = lax.broadcasted_iota(jnp.int32, (quad, quad), 0)
    l4 = lax.broadcasted_iota(jnp.int32, (quad, quad), 1)
    bd4_mask = (r4 // RW_HEAD) == (l4 // RW_HEAD)
    r2 = lax.broadcasted_iota(jnp.int32, (2 * CHUNK, LANES), 0)
    l2 = lax.broadcasted_iota(jnp.int32, (2 * CHUNK, LANES), 1)
    bd_mask = (r2 < CHUNK) == (l2 < RW_HEAD)
    eye = (r2 == l2).astype(F32)
    eye2 = jnp.concatenate([eye, eye], axis=0)
    z_pair = jnp.zeros((2 * CHUNK, LANES), F32)
    z_half = jnp.zeros((CHUNK, LANES), F32)

    def bd(x):
        return jnp.where(bd_mask, jnp.concatenate([x, x], axis=0), 0.0)

    def bd4(x):
        return jnp.where(bd4_mask, jnp.concatenate([x, x, x, x], axis=0), 0.0)

    def blockdiag2(xa, xb):
        return jnp.concatenate([jnp.concatenate([xa, z_pair], axis=1),
                                jnp.concatenate([z_pair, xb], axis=1)], axis=0)

    def stacked_bd(st):
        return blockdiag2(st[:2 * CHUNK], st[2 * CHUNK:])

    items = [(c, qd) for c in range(nchunk) for qd in range(RW_WIDTH // quad)]

    def view(name, item):
        c, qd = item
        return stage_ref[_STAGED.index(name), c * CHUNK:(c + 1) * CHUNK, qd * quad:(qd + 1) * quad]

    a2 = [_dot_nt(jnp.concatenate([view(a_t, it), view(r_t, it)], axis=0).astype(BF16),
                  jnp.concatenate([bd4(view(b_t, it)), bd4(view(k_t, it))], axis=0).astype(BF16))
          for it in items]
    yield
    a_ab = [jnp.where(strict, z[:CHUNK, :quad], 0.0) for z in a2]
    a_ak = [jnp.where(strict, z[:CHUNK, quad:], 0.0) for z in a2]
    a_rb = [jnp.where(incl, z[CHUNK:, :quad], 0.0) for z in a2]
    a_rk = [jnp.where(incl, z[CHUNK:, quad:], 0.0) for z in a2]

    n_st = [jnp.concatenate([bd(z[:, :LANES]), bd(z[:, LANES:])], axis=0) for z in a_ab]
    t_st = [eye2 + n for n in n_st]
    p_st = [_dot(stacked_bd(n).astype(BF16), n.astype(BF16)) for n in n_st]
    yield
    for _ in range(4):
        zz = [_dot(stacked_bd(p).astype(BF16), jnp.concatenate([p, t], axis=1).astype(BF16))
              for p, t in zip(p_st, t_st)]
        p_st = [z[:, :LANES] for z in zz]
        t_st = [t + z[:, LANES:] for t, z in zip(t_st, zz)]
        yield
    t_st = [t + _dot(stacked_bd(p).astype(BF16), t.astype(BF16)) for p, t in zip(p_st, t_st)]
    yield

    bd4_v = [bd4(view(v, it)) for it in items]
    av = [_dot(x.astype(BF16), bv.astype(BF16)) for x, bv in zip(a_ak, bd4_v)]
    x_st = [jnp.concatenate(
        [jnp.concatenate([bd(view(a_t, it)[:, :LANES]), bd(u[:, :LANES])], axis=1),
         jnp.concatenate([bd(view(a_t, it)[:, LANES:]), bd(u[:, LANES:])], axis=1)], axis=0)
        for it, u in zip(items, av)]
    tx = [_dot(stacked_bd(t).astype(BF16), x.astype(BF16)) for t, x in zip(t_st, x_st)]
    yield

    for n_item, (it, tx_i, arb, ark, bv4) in enumerate(zip(items, tx, a_rb, a_rk, bd4_v)):
        if n_item and n_item % 2 == 0:
            yield
        c, qd = it
        r_bar, y_bar, au = [], [], []
        for half in range(2):
            pl_ = slice(half * LANES, (half + 1) * LANES)
            rows = slice(half * 2 * CHUNK, (half + 1) * 2 * CHUNK)
            tx_p = tx_i[rows]
            rhs = jnp.concatenate(
                [tx_p, jnp.concatenate([z_pair, bv4[rows, pl_]], axis=1)], axis=0)
            ry = _dot(jnp.concatenate([arb[:, pl_], ark[:, pl_]], axis=1).astype(BF16),
                      rhs.astype(BF16))
            r_bar.append(view(r_t, it)[:, pl_] + ry[:, :LANES])
            y_bar.append(ry[:, LANES:])
            au.append(tx_p[:CHUNK] + tx_p[CHUNK:])
        v_q = view(v, it)
        mg_rhs = jnp.concatenate(
            [au[0], jnp.concatenate([z_half, v_q[:, :LANES]], axis=1),
             au[1], jnp.concatenate([z_half, v_q[:, LANES:]], axis=1)], axis=0)
        lhs_t = jnp.concatenate([view(b_h, it), view(k_h, it)], axis=0).T
        mg = _dot(stacked_bd(lhs_t).astype(BF16), mg_rhs.astype(BF16))
        m_sel = jnp.concatenate([mg[0:CHUNK], mg[2 * CHUNK:3 * CHUNK]], axis=1)
        m_odd = jnp.concatenate([mg[CHUNK:2 * CHUNK], mg[3 * CHUNK:]], axis=1)
        m_even_odd = jnp.where(jnp.concatenate([even_head, even_head], axis=1), m_sel, m_odd)
        m_quad = (jnp.concatenate([m_even_odd[:, :LANES], m_even_odd[:, 2 * LANES:3 * LANES]], axis=1)
                  + jnp.where(diag, view(gam_c, it), 0.0))
        g_quad = jnp.concatenate([m_even_odd[:, LANES:2 * LANES], m_even_odd[:, 3 * LANES:]], axis=1)
        cols = slice(qd * quad, (qd + 1) * quad)
        rm_ref[c, :, cols] = jnp.concatenate(
            [jnp.concatenate(r_bar, axis=1), m_quad], axis=0).astype(rm_ref.dtype)
        yg_ref[c, :, cols] = jnp.concatenate([jnp.concatenate(y_bar, axis=1), g_quad], axis=0)


def _rwkv_prep_kernel(rw_ref, prev_ref, mu_ref, w0_ref, a0_ref, kk_ref, ka_ref, rk_ref,
                      wa2_ref, g2_ref, ones_ref, tri_ref,
                      rm_ref, yg_ref, bonus_ref, g_ref, stage_a, stage_b,
                      *, ntiles, tiles_per_seq):
    t = pl.program_id(0)
    tile = jnp.minimum(t, ntiles - 1)
    seq_start = (tile % tiles_per_seq) == 0

    @pl.when(t == 0)
    def _():
        stage_b[...] = jnp.zeros(stage_b.shape, F32)

    def step(write_ref, read_ref):
        pending = [
            _rwkv_chain(read_ref, rm_ref, yg_ref),
            _rwkv_stage(seq_start, rw_ref, prev_ref, mu_ref, w0_ref, a0_ref, kk_ref, ka_ref,
                        rk_ref, wa2_ref, g2_ref, ones_ref, tri_ref, bonus_ref, g_ref, write_ref),
        ]
        while pending:
            for gen in list(pending):
                try:
                    next(gen)
                except StopIteration:
                    pending.remove(gen)

    pl.when(t % 2 == 0)(functools.partial(step, stage_a, stage_b))
    pl.when(t % 2 == 1)(functools.partial(step, stage_b, stage_a))


def _rwkv_prep(rw, mu, w0, a0, k_k, k_a, r_k, wa2, g2, ones_bd, tri, seq):
    m = rw.shape[0]
    tt = RW_PREP_ROWS
    cpt = tt // CHUNK
    ntiles = m // tt
    cur = lambda t: jnp.minimum(t, ntiles - 1)
    done = lambda t: jnp.maximum(t - 1, 0)
    vec = lambda n: pl.BlockSpec((1, n), lambda t: (0, 0))
    return pl.pallas_call(
        functools.partial(_rwkv_prep_kernel, ntiles=ntiles, tiles_per_seq=seq // tt),
        grid=(ntiles + 1,),
        in_specs=[
            pl.BlockSpec((tt, RW_IN_WIDTH), lambda t: (cur(t), 0)),
            pl.BlockSpec((8, RW_IN_WIDTH), lambda t: (jnp.maximum(cur(t) * (tt // 8) - 1, 0), 0)),
            vec(RW_IN_WIDTH), vec(RW_WIDTH), vec(RW_WIDTH), vec(RW_WIDTH), vec(RW_WIDTH),
            vec(RW_WIDTH),
            pl.BlockSpec((LANES, 2 * RW_WIDTH), lambda t: (0, 0)),
            pl.BlockSpec((RW_GATE_LORA, RW_WIDTH), lambda t: (0, 0)),
            pl.BlockSpec((SEG_TILE, SEG_TILE), lambda t: (0, 0)),
            pl.BlockSpec((tt, tt), lambda t: (0, 0)),
        ],
        out_specs=[
            pl.BlockSpec((cpt, 2 * CHUNK, RW_WIDTH), lambda t: (done(t), 0, 0)),
            pl.BlockSpec((cpt, 2 * CHUNK, RW_WIDTH), lambda t: (done(t), 0, 0)),
            pl.BlockSpec((tt, RW_WIDTH), lambda t: (cur(t), 0)),
            pl.BlockSpec((tt, RW_WIDTH), lambda t: (cur(t), 0)),
        ],
        out_shape=[
            jax.ShapeDtypeStruct((m // CHUNK, 2 * CHUNK, RW_WIDTH), BF16),
            jax.ShapeDtypeStruct((m // CHUNK, 2 * CHUNK, RW_WIDTH), F32),
            jax.ShapeDtypeStruct((m, RW_WIDTH), F32),
            jax.ShapeDtypeStruct((m, RW_WIDTH), F32),
        ],
        scratch_shapes=[
            pltpu.VMEM((len(_STAGED), tt, RW_WIDTH), F32),
            pltpu.VMEM((len(_STAGED), tt, RW_WIDTH), F32),
        ],
        compiler_params=pltpu.CompilerParams(
            dimension_semantics=("arbitrary",), vmem_limit_bytes=VMEM_LIMIT_BYTES),
        name="rwkv_prep",
    )(rw, rw, mu, w0, a0, k_k, k_a, r_k, wa2, g2, ones_bd, tri)


def _rwkv_scan_kernel(rm_ref, yg_ref, bonus_ref, g_ref, lng_ref, lnb_ref, ones_ref,
                      y_ref, h_sc, y_sc):
    c = pl.program_id(0)
    nb = rm_ref.shape[0]

    @pl.when(c == 0)
    def _():
        h_sc[...] = jnp.zeros(h_sc.shape, F32)

    r128 = lax.broadcasted_iota(jnp.int32, (2 * CHUNK, LANES), 0)
    l128 = lax.broadcasted_iota(jnp.int32, (2 * CHUNK, LANES), 1)
    bd_mask = (r128 < CHUNK) == (l128 < RW_HEAD)

    for bi in range(nb):
        for p in range(RW_PAIRS):
            lanes = slice(p * LANES, (p + 1) * LANES)
            h_bd = h_sc[bi * RW_PAIRS + p]
            out = _dot(rm_ref[bi, :, lanes], h_bd.astype(BF16)) + yg_ref[bi, :, lanes]
            y_sc[bi, :, lanes] = out[:CHUNK]
            h_new = out[CHUNK:]
            h_sc[bi * RW_PAIRS + p] = jnp.where(
                bd_mask, jnp.concatenate([h_new, h_new], axis=0), 0.0)

    ones_bd = ones_ref[...]
    y = y_sc[...].reshape(nb * CHUNK, RW_WIDTH)
    mean = _segsum(y, ones_bd) * (1.0 / RW_HEAD)
    d = y - mean
    var = _segsum(d * d, ones_bd) * (1.0 / RW_HEAD)
    yn = d * lax.rsqrt(var + RW_GN_EPS) * lng_ref[...] + lnb_ref[...]
    bonus = bonus_ref[...].reshape(nb * CHUNK, RW_WIDTH)
    gate = g_ref[...].reshape(nb * CHUNK, RW_WIDTH)
    y_ref[...] = ((yn + bonus) * gate).reshape(nb, CHUNK, RW_WIDTH).astype(y_ref.dtype)


def _rwkv_scan(rm, yg, bonus, g, ln_g, ln_b, ones_bd):
    b, nc, _, _ = rm.shape
    s = nc * CHUNK
    vec = lambda n: pl.BlockSpec((1, n), lambda ci: (0, 0))
    return pl.pallas_call(
        _rwkv_scan_kernel,
        grid=(nc,),
        in_specs=[
            pl.BlockSpec((b, None, 2 * CHUNK, RW_WIDTH), lambda ci: (0, ci, 0, 0)),
            pl.BlockSpec((b, None, 2 * CHUNK, RW_WIDTH), lambda ci: (0, ci, 0, 0)),
            pl.BlockSpec((b, CHUNK, RW_WIDTH), lambda ci: (0, ci, 0)),
            pl.BlockSpec((b, CHUNK, RW_WIDTH), lambda ci: (0, ci, 0)),
            vec(RW_WIDTH), vec(RW_WIDTH),
            pl.BlockSpec((SEG_TILE, SEG_TILE), lambda ci: (0, 0)),
        ],
        out_specs=pl.BlockSpec((b, CHUNK, RW_WIDTH), lambda ci: (0, ci, 0)),
        out_shape=jax.ShapeDtypeStruct((b, s, RW_WIDTH), BF16),
        scratch_shapes=[
            pltpu.VMEM((b * RW_PAIRS, 2 * CHUNK, LANES), F32),
            pltpu.VMEM((b, CHUNK, RW_WIDTH), F32),
        ],
        compiler_params=pltpu.CompilerParams(
            dimension_semantics=("arbitrary",), vmem_limit_bytes=VMEM_LIMIT_BYTES),
        name="rwkv_scan",
    )(rm, yg, bonus, g, ln_g, ln_b, ones_bd)


def _merge_ffn_kernel(x_ref, ya_ref, yb_ref, gate_ref, wa_ref, wb_ref, wo_ref,
                      gffn_ref, w1_ref, w2_ref, gfin_ref, o_ref):
    pa = _dot(ya_ref[...], wa_ref[...])
    pb = _dot(yb_ref[...], wb_ref[...])
    gate = gate_ref[...]
    merged = (jax.nn.sigmoid(gate[:, :D_MODEL]) * pa
              + jax.nn.sigmoid(gate[:, D_MODEL:]) * pb)
    x1 = x_ref[...] + _dot(merged.astype(BF16), wo_ref[...])
    ms = jnp.mean(x1 * x1, axis=-1, keepdims=True)
    h = (x1 * lax.rsqrt(ms + NORM_EPS) * gffn_ref[...]).astype(BF16)
    acc = x1
    for c in range(D_FF // FFN_CHUNK):
        cols = slice(c * FFN_CHUNK, (c + 1) * FFN_CHUNK)
        f = jnp.maximum(_dot(h, w1_ref[:, cols]), 0.0)
        acc = acc + _dot((f * f).astype(BF16), w2_ref[cols, :])
    ms2 = jnp.mean(acc * acc, axis=-1, keepdims=True)
    o_ref[...] = acc * lax.rsqrt(ms2 + NORM_EPS) * gfin_ref[...]


def _merge_ffn(x2d, ya, yb, gates, wa, wb, wo, gffn, w1, w2, gfin):
    m = x2d.shape[0]
    tm = FFN_ROWS
    row = lambda i: (i, 0)
    return pl.pallas_call(
        _merge_ffn_kernel,
        grid=(m // tm,),
        in_specs=[
            pl.BlockSpec((tm, D_MODEL), row),
            pl.BlockSpec((tm, DA_WIDTH), row),
            pl.BlockSpec((tm, RW_WIDTH), row),
            pl.BlockSpec((tm, GATE_WIDTH), row),
            _resident((DA_WIDTH, D_MODEL)),
            _resident((RW_WIDTH, D_MODEL)),
            _resident((D_MODEL, D_MODEL)),
            _resident((1, D_MODEL)),
            _resident((D_MODEL, D_FF)),
            _resident((D_FF, D_MODEL)),
            _resident((1, D_MODEL)),
        ],
        out_specs=pl.BlockSpec((tm, D_MODEL), row),
        out_shape=jax.ShapeDtypeStruct((m, D_MODEL), F32),
        compiler_params=pltpu.CompilerParams(
            dimension_semantics=("parallel",), vmem_limit_bytes=VMEM_LIMIT_BYTES),
        name="merge_ffn",
    )(x2d, ya, yb, gates, wa, wb, wo, gffn, w1, w2, gfin)


def _rope_tables(seq):
    d = DA_HALF_DIM
    pos = jnp.arange(seq, dtype=F32)
    inv_freq = ROPE_THETA ** (-jnp.arange(0, d, 2, dtype=F32) / d)
    ang = pos[:, None] * inv_freq[None, :]
    cos = jnp.cos(ang)
    sin = jnp.sin(ang)
    reps = LANES // d
    cos_full = jnp.tile(jnp.concatenate([cos, cos], axis=-1), (1, reps))
    sin_full = jnp.tile(jnp.concatenate([-sin, sin], axis=-1), (1, reps))
    return cos_full, sin_full


def kernel(x, norm_mix_g, w_in, rw_mu, rw_w0, rw_w2, rw_a0, rw_a2, rw_g2, rw_k_k, rw_k_a,
           rw_r_k, rw_ln_g, rw_ln_b, da_lq1, da_lk1, da_lq2, da_lk2, da_subln_g,
           w_branch_a, w_branch_b, w_o, norm_ffn_g, w_ff1, w_ff2, norm_final_g):
    b, s, d = x.shape
    assert d == D_MODEL and norm_mix_g.shape[0] == 1
    assert s % ATTN_BLOCK == 0 and s % INPROJ_ROWS == 0 and s % RW_PREP_ROWS == 0
    assert (b * s) % FFN_ROWS == 0
    row = lambda t: t.reshape(1, -1)
    x2d = x.reshape(b * s, d)

    cos, sin = _rope_tables(s)
    q, k, v, rw, gates = _inproj(x2d, norm_mix_g, w_in[0].astype(BF16), cos, sin, s)

    ya = _attention(q.reshape(b, s, DA_WIDTH), k.reshape(b, s, DA_WIDTH),
                    v.reshape(b, s, DA_WIDTH), row(da_lq1), row(da_lk1), row(da_lq2),
                    row(da_lk2), row(da_subln_g))

    wa2 = jnp.zeros((LANES, 2 * RW_WIDTH), F32)
    wa2 = wa2.at[:RW_DECAY_LORA, :RW_WIDTH].set(rw_w2[0])
    wa2 = wa2.at[RW_DECAY_LORA:, RW_WIDTH:].set(rw_a2[0])
    idx = jnp.arange(SEG_TILE) // RW_HEAD
    ones_bd = (idx[:, None] == idx[None, :]).astype(BF16)
    tok = jnp.arange(RW_PREP_ROWS)
    tri = ((tok[:, None] >= tok[None, :])
           & (tok[:, None] // CHUNK == tok[None, :] // CHUNK)).astype(BF16)

    rm, yg, bonus, g = _rwkv_prep(
        rw, rw_mu, rw_w0, rw_a0, rw_k_k, rw_k_a, row(rw_r_k),
        wa2.astype(BF16), rw_g2[0].astype(BF16), ones_bd, tri, s)
    nc = s // CHUNK
    yb = _rwkv_scan(rm.reshape(b, nc, 2 * CHUNK, RW_WIDTH), yg.reshape(b, nc, 2 * CHUNK, RW_WIDTH),
                    bonus.reshape(b, s, RW_WIDTH), g.reshape(b, s, RW_WIDTH),
                    rw_ln_g, rw_ln_b, ones_bd)

    out = _merge_ffn(x2d, ya.reshape(b * s, DA_WIDTH), yb.reshape(b * s, RW_WIDTH), gates,
                     w_branch_a[0].astype(BF16), w_branch_b[0].astype(BF16),
                     w_o[0].astype(BF16), norm_ffn_g, w_ff1[0].astype(BF16),
                     w_ff2[0].astype(BF16), row(norm_final_g))
    return out.reshape(b, s, d)
```

```python
import functools
import math

import jax
import jax.numpy as jnp
from jax import lax
from jax.experimental import pallas as pl
from jax.experimental.pallas import tpu as pltpu

F32 = jnp.float32
BF16 = jnp.bfloat16

D_MODEL = 1024
CHUNK = 64
NORM_EPS = 1e-6
ROPE_THETA = 10000.0
DA_WIDTH = 512
DA_HEADS = 4
DA_HALF_DIM = 64
DA_VDIM = 128
DA_SUBLN_EPS = 1e-5
RW_WIDTH = 512
RW_HEAD = 64
RW_PAIRS = RW_WIDTH // (2 * RW_HEAD)
RW_DECAY_LORA = 64
RW_AAA_LORA = 64
RW_GATE_LORA = 128
RW_GN_EPS = RW_HEAD * 1e-5
RW_IN_WIDTH = 3 * RW_WIDTH + RW_DECAY_LORA + RW_AAA_LORA + RW_GATE_LORA
DA_IN_WIDTH = 3 * DA_WIDTH
GATE_WIDTH = 2 * D_MODEL
D_IN = DA_IN_WIDTH + RW_IN_WIDTH + GATE_WIDTH
D_FF = 4 * D_MODEL
LAMBDA_INIT = 0.8 - 0.6 * math.exp(0.0)
Q_SCALE = DA_HALF_DIM ** -0.5 * math.log2(math.e)

LANES = 128
MXU_TILE = 256
SEG_TILE = MXU_TILE
VMEM_LIMIT_BYTES = 56 * 1024 * 1024

INPROJ_ROWS = 512
ATTN_BLOCK = 256
ATTN_BATCH_GROUP = 1
RW_PREP_ROWS = 256
SCAN_CHUNKS = 2
FFN_ROWS = 512
FFN_CHUNK = 1024


def _dot(a, b):
    return jnp.dot(a, b, preferred_element_type=F32)


def _dot_nt(a, b):
    return lax.dot_general(a, b, (((1,), (1,)), ((), ())), preferred_element_type=F32)


def _split2(x):
    hi = x.astype(BF16)
    lo = (x - hi.astype(F32)).astype(BF16)
    return hi, lo


def _segsum(x, ones_bd):
    hi, lo = _split2(x)
    w = ones_bd.shape[0]
    parts = [_dot(hi[:, c:c + w], ones_bd) + _dot(lo[:, c:c + w], ones_bd)
             for c in range(0, x.shape[1], w)]
    return jnp.concatenate(parts, axis=1)


def _resident(shape):
    nd = len(shape)
    return pl.BlockSpec(shape, lambda *_: (0,) * nd, pipeline_mode=pl.Buffered(1))


def _inproj_kernel(x_ref, g_ref, w_ref, cos_ref, sin_ref,
                   q_ref, k_ref, v_ref, rw_ref, gate_ref):
    x = x_ref[...]
    ms = jnp.mean(x * x, axis=-1, keepdims=True)
    h = (x * lax.rsqrt(ms + NORM_EPS) * g_ref[...]).astype(BF16)

    def proj(c0, c1):
        return _dot(h, w_ref[:, c0:c1])

    rows = x.shape[0]
    lane = lax.broadcasted_iota(jnp.int32, (rows, DA_WIDTH), 1)
    first_half = (lane & (DA_HALF_DIM - 1)) < (DA_HALF_DIM // 2)

    reps = DA_WIDTH // LANES
    cos = jnp.concatenate([cos_ref[...]] * reps, axis=1)
    sin = jnp.concatenate([sin_ref[...]] * reps, axis=1)

    def rope(t):
        partner = jnp.where(first_half,
                            pltpu.roll(t, DA_WIDTH - DA_HALF_DIM // 2, 1),
                            pltpu.roll(t, DA_HALF_DIM // 2, 1))
        return t * cos + partner * sin

    q_ref[...] = (rope(proj(0, DA_WIDTH)) * Q_SCALE).astype(q_ref.dtype)
    k_ref[...] = rope(proj(DA_WIDTH, 2 * DA_WIDTH)).astype(k_ref.dtype)
    v_ref[...] = proj(2 * DA_WIDTH, 3 * DA_WIDTH).astype(v_ref.dtype)
    rw_ref[...] = proj(DA_IN_WIDTH, DA_IN_WIDTH + RW_IN_WIDTH)
    gate_ref[...] = proj(DA_IN_WIDTH + RW_IN_WIDTH, D_IN)


def _inproj(x2d, g, w_bf, cos, sin, seq):
    m = x2d.shape[0]
    tm = INPROJ_ROWS
    nseq = seq // tm
    row = lambda i: (i, 0)
    tab = lambda i: (i % nseq, 0)
    return pl.pallas_call(
        _inproj_kernel,
        grid=(m // tm,),
        in_specs=[
            pl.BlockSpec((tm, D_MODEL), row),
            _resident((1, D_MODEL)),
            _resident((D_MODEL, D_IN)),
            pl.BlockSpec((tm, LANES), tab),
            pl.BlockSpec((tm, LANES), tab),
        ],
        out_specs=[
            pl.BlockSpec((tm, DA_WIDTH), row),
            pl.BlockSpec((tm, DA_WIDTH), row),
            pl.BlockSpec((tm, DA_WIDTH), row),
            pl.BlockSpec((tm, RW_IN_WIDTH), row),
            pl.BlockSpec((tm, GATE_WIDTH), row),
        ],
        out_shape=[
            jax.ShapeDtypeStruct((m, DA_WIDTH), BF16),
            jax.ShapeDtypeStruct((m, DA_WIDTH), BF16),
            jax.ShapeDtypeStruct((m, DA_WIDTH), BF16),
            jax.ShapeDtypeStruct((m, RW_IN_WIDTH), F32),
            jax.ShapeDtypeStruct((m, GATE_WIDTH), F32),
        ],
        compiler_params=pltpu.CompilerParams(
            dimension_semantics=("parallel",), vmem_limit_bytes=VMEM_LIMIT_BYTES),
        name="inproj",
    )(x2d, g, w_bf, cos, sin)


def _attn_kernel(lq1_ref, lk1_ref, lq2_ref, lk2_ref, sg_ref, q_ref, k_ref, v_ref,
                 o_ref):
    nbatch, tq, width = q_ref.shape
    heads = width // DA_VDIM
    i = pl.program_id(1)
    lane = lax.broadcasted_iota(jnp.int32, (tq, DA_VDIM), 1)
    lam = (jnp.exp(jnp.sum(lq1_ref[...] * lk1_ref[...], axis=-1, keepdims=True))
           - jnp.exp(jnp.sum(lq2_ref[...] * lk2_ref[...], axis=-1, keepdims=True))
           + LAMBDA_INIT)
    r = lax.broadcasted_iota(jnp.int32, (2 * tq, tq), 0)
    c = lax.broadcasted_iota(jnp.int32, (2 * tq, tq), 1)
    allowed = (c // CHUNK) <= ((r & (tq - 1)) // CHUNK)

    def attend(nfull, bi, h):
        cols = slice(h * DA_VDIM, (h + 1) * DA_VDIM)
        q = q_ref[bi, :, cols]
        zero = jnp.zeros_like(q)
        qq = jnp.concatenate([jnp.where(lane < DA_HALF_DIM, q, zero),
                              jnp.where(lane >= DA_HALF_DIM, q, zero)], axis=0)
        s = []
        for j in range(nfull + 1):
            s_j = _dot_nt(qq, k_ref[bi, j * tq:(j + 1) * tq, cols])
            s.append(jnp.where(allowed, s_j, -jnp.inf) if j == nfull else s_j)
            yield
        m_lane = s[0]
        for s_j in s[1:]:
            m_lane = jnp.maximum(m_lane, s_j)
        m = jnp.max(m_lane, axis=-1, keepdims=True)
        l_lane = None
        acc = None
        for j, s_j in enumerate(s):
            p = jnp.exp2(s_j - m)
            pv = _dot(p.astype(BF16), v_ref[bi, j * tq:(j + 1) * tq, cols])
            l_lane = p if l_lane is None else l_lane + p
            acc = pv if acc is None else acc + pv
            if j < nfull:
                yield
        l = jnp.sum(l_lane, axis=-1, keepdims=True)
        o = acc / l
        od = o[:tq] - lam * o[tq:]
        ms = jnp.mean(od * od, axis=-1, keepdims=True)
        y = od * lax.rsqrt(ms + DA_SUBLN_EPS) * sg_ref[...]
        o_ref[bi, :, cols] = (y * (1.0 - LAMBDA_INIT)).astype(o_ref.dtype)

    def variant(nfull):
        gens = [attend(nfull, bi, h) for bi in range(nbatch) for h in range(heads)]
        nscore = nfull + 1
        for _ in range(nscore):
            next(gens[0])
        for k, cur in enumerate(gens):
            ahead = gens[k + 1] if k + 1 < len(gens) else None
            left = nscore if ahead is not None else 0
            for _ in cur:
                if left:
                    next(ahead)
                    left -= 1
            for _ in range(left):
                next(ahead)

    for n in range(k_ref.shape[1] // tq):
        pl.when(i == n)(functools.partial(variant, n))


def _attention(q, k, v, lq1, lk1, lq2, lk2, subln_g):
    b, s, _ = q.shape
    tq = ATTN_BLOCK
    assert tq & (tq - 1) == 0 and s % tq == 0
    vec = lambda n: pl.BlockSpec((1, n), lambda bi, qi: (0, 0))
    nb = ATTN_BATCH_GROUP
    assert b % nb == 0
    return pl.pallas_call(
        _attn_kernel,
        grid=(b // nb, s // tq),
        in_specs=[
            vec(DA_HALF_DIM), vec(DA_HALF_DIM), vec(DA_HALF_DIM), vec(DA_HALF_DIM),
            vec(DA_VDIM),
            pl.BlockSpec((nb, tq, DA_WIDTH), lambda bi, qi: (bi, qi, 0)),
            pl.BlockSpec((nb, s, DA_WIDTH), lambda bi, qi: (bi, 0, 0)),
            pl.BlockSpec((nb, s, DA_WIDTH), lambda bi, qi: (bi, 0, 0)),
        ],
        out_specs=pl.BlockSpec((nb, tq, DA_WIDTH), lambda bi, qi: (bi, qi, 0)),
        out_shape=jax.ShapeDtypeStruct((b, s, DA_WIDTH), BF16),
        compiler_params=pltpu.CompilerParams(
            dimension_semantics=("parallel", "arbitrary"),
            vmem_limit_bytes=VMEM_LIMIT_BYTES),
        name="diff_attn",
    )(lq1, lk1, lq2, lk2, subln_g, q, k, v)


_STAGED = ("r_t", "a_t", "k_t", "b_t", "k_h", "b_h", "v", "gam_c")


def _rwkv_stage(seq_start, rw_ref, prev_ref, mu_ref, w0_ref, a0_ref, kk_ref, ka_ref, rk_ref,
                wa2_ref, g2_ref, ones_ref, tri_ref, bonus_ref, g_ref, stage_ref):
    tt = rw_ref.shape[0]
    z = rw_ref[...]
    prev = jnp.where(seq_start, 0.0, prev_ref[7:8, :])
    row = lax.broadcasted_iota(jnp.int32, z.shape, 0)
    z_prev = jnp.where(row == 0, prev, pltpu.roll(z, 1, 0))
    zs = z + (z_prev - z) * mu_ref[...]
    yield

    w3 = 3 * RW_WIDTH
    r = zs[:, 0:RW_WIDTH]
    k = zs[:, RW_WIDTH:2 * RW_WIDTH]
    v = zs[:, 2 * RW_WIDTH:w3]
    x_wa = zs[:, w3:w3 + LANES]
    x_g = zs[:, w3 + LANES:w3 + 2 * LANES]
    lane = lax.broadcasted_iota(jnp.int32, x_wa.shape, 1)
    t_wa = jnp.where(lane < RW_DECAY_LORA, jnp.tanh(x_wa), x_wa)
    lora = _dot(t_wa.astype(BF16), wa2_ref[...])
    u = -(w0_ref[...] + lora[:, :RW_WIDTH])
    softplus = jnp.maximum(u, 0.0) + jnp.log(1.0 + jnp.exp(-jnp.abs(u)))
    log_decay = -jnp.exp(-softplus - 0.5)
    yield
    a_lr = jax.nn.sigmoid(a0_ref[...] + lora[:, RW_WIDTH:])
    g_ref[...] = _dot(jax.nn.sigmoid(x_g).astype(BF16), g2_ref[...])
    yield

    ones_bd = ones_ref[...]
    kk = k * kk_ref[...]
    kk = kk * jnp.minimum(lax.rsqrt(_segsum(kk * kk, ones_bd)), 1e12)
    yield
    k2 = k * (1.0 + (a_lr - 1.0) * ka_ref[...])
    a_vec = -kk
    b_vec = kk * a_lr
    bonus_ref[...] = _segsum(r * k2 * rk_ref[...], ones_bd) * v
    yield

    tri = tri_ref[...]
    h1 = log_decay.astype(BF16)
    r1 = log_decay - h1.astype(F32)
    h2 = r1.astype(BF16)
    h3 = (r1 - h2.astype(F32)).astype(BF16)
    cs = _dot(tri, h1) + _dot(tri, h2) + _dot(tri, h3)
    nchunk = tt // CHUNK
    cl = jnp.concatenate(
        [jnp.broadcast_to(cs[(c + 1) * CHUNK - 1:(c + 1) * CHUNK, :], (CHUNK, RW_WIDTH))
         for c in range(nchunk)], axis=0)
    yield
    stage_ref[_STAGED.index("v")] = v
    stage_ref[_STAGED.index("gam_c")] = jnp.exp(cl)
    stage_ref[_STAGED.index("r_t")] = r * jnp.exp(cs)
    yield
    stage_ref[_STAGED.index("a_t")] = a_vec * jnp.exp(cs - log_decay)
    yield
    gam_inv = jnp.exp(-cs)
    stage_ref[_STAGED.index("k_t")] = k2 * gam_inv
    stage_ref[_STAGED.index("b_t")] = b_vec * gam_inv
    yield
    gam_end = jnp.exp(cl - cs)
    stage_ref[_STAGED.index("k_h")] = k2 * gam_end
    stage_ref[_STAGED.index("b_h")] = b_vec * gam_end


def _rwkv_chain(stage_ref, rm_ref, yg_ref):
    r_t, a_t, k_t, b_t, k_h, b_h, v, gam_c = _STAGED
    tt = stage_ref.shape[1]
    nchunk = tt // CHUNK

    quad = 2 * LANES
    cq = (CHUNK, quad)
    t_idx = lax.broadcasted_iota(jnp.int32, cq, 0)
    l_idx = lax.broadcasted_iota(jnp.int32, cq, 1)
    s_idx = l_idx & (RW_HEAD - 1)
    strict = s_idx < t_idx
    incl = s_idx <= t_idx
    diag = s_idx == t_idx
    even_head = (l_idx & RW_HEAD) == 0
    r4 = lax.broadcasted_iota(jnp.int32, (quad, quad), 0)
    l4 = lax.broadcasted_iota(jnp.int32, (quad, quad), 1)
    bd4_mask = (r4 // RW_HEAD) == (l4 // RW_HEAD)
    r2 = lax.broadcasted_iota(jnp.int32, (2 * CHUNK, LANES), 0)
    l2 = lax.broadcasted_iota(jnp.int32, (2 * CHUNK, LANES), 1)
    bd_mask = (r2 < CHUNK) == (l2 < RW_HEAD)
    eye = (r2 == l2).astype(F32)
    eye2 = jnp.concatenate([eye, eye], axis=0)
    z_pair = jnp.zeros((2 * CHUNK, LANES), F32)
    z_half = jnp.zeros((CHUNK, LANES), F32)

    def bd(x):
        return jnp.where(bd_mask, jnp.concatenate([x, x], axis=0), 0.0)

    def bd4(x):
        return jnp.where(bd4_mask, jnp.concatenate([x, x, x, x], axis=0), 0.0)

    def blockdiag2(xa, xb):
        zero = z_pair.astype(xa.dtype)
        return jnp.concatenate([jnp.concatenate([xa, zero], axis=1),
                                jnp.concatenate([zero, xb], axis=1)], axis=0)

    def stacked_bd(st):
        return blockdiag2(st[:2 * CHUNK], st[2 * CHUNK:])

    items = [(c, qd) for c in range(nchunk) for qd in range(RW_WIDTH // quad)]

    def view(name, item):
        c, qd = item
        return stage_ref[_STAGED.index(name), c * CHUNK:(c + 1) * CHUNK, qd * quad:(qd + 1) * quad]

    a2 = [_dot_nt(jnp.concatenate([view(a_t, it), view(r_t, it)], axis=0).astype(BF16),
                  jnp.concatenate([bd4(view(b_t, it)), bd4(view(k_t, it))], axis=0).astype(BF16))
          for it in items]
    yield
    a_ab = [jnp.where(strict, z[:CHUNK, :quad], 0.0) for z in a2]
    a_ak = [jnp.where(strict, z[:CHUNK, quad:], 0.0) for z in a2]
    a_rb = [jnp.where(incl, z[CHUNK:, :quad], 0.0) for z in a2]
    a_rk = [jnp.where(incl, z[CHUNK:, quad:], 0.0) for z in a2]

    n_st = [jnp.concatenate([bd(z[:, :LANES]), bd(z[:, LANES:])], axis=0) for z in a_ab]
    t_st = [eye2 + n for n in n_st]
    n_bf = [n.astype(BF16) for n in n_st]
    p_st = [_dot(stacked_bd(n), n) for n in n_bf]
    yield
    for _ in range(4):
        p_bf = [p.astype(BF16) for p in p_st]
        zz = [_dot(stacked_bd(p), jnp.concatenate([p, t.astype(BF16)], axis=1))
              for p, t in zip(p_bf, t_st)]
        p_st = [z[:, :LANES] for z in zz]
        t_st = [t + z[:, LANES:] for t, z in zip(t_st, zz)]
        yield
    t_st = [t + _dot(stacked_bd(p.astype(BF16)), t.astype(BF16)) for p, t in zip(p_st, t_st)]
    yield

    bd4_v = [bd4(view(v, it)) for it in items]
    av = [_dot(x.astype(BF16), bv.astype(BF16)) for x, bv in zip(a_ak, bd4_v)]
    x_st = [jnp.concatenate(
        [jnp.concatenate([bd(view(a_t, it)[:, :LANES]), bd(u[:, :LANES])], axis=1),
         jnp.concatenate([bd(view(a_t, it)[:, LANES:]), bd(u[:, LANES:])], axis=1)], axis=0)
        for it, u in zip(items, av)]
    tx = [_dot(stacked_bd(t).astype(BF16), x.astype(BF16)) for t, x in zip(t_st, x_st)]
    yield

    for n_item, (it, tx_i, arb, ark, bv4) in enumerate(zip(items, tx, a_rb, a_rk, bd4_v)):
        if n_item and n_item % 2 == 0:
            yield
        c, qd = it
        r_bar, y_bar, au = [], [], []
        for half in range(2):
            pl_ = slice(half * LANES, (half + 1) * LANES)
            rows = slice(half * 2 * CHUNK, (half + 1) * 2 * CHUNK)
            tx_p = tx_i[rows]
            rhs = jnp.concatenate(
                [tx_p, jnp.concatenate([z_pair, bv4[rows, pl_]], axis=1)], axis=0)
            ry = _dot(jnp.concatenate([arb[:, pl_], ark[:, pl_]], axis=1).astype(BF16),
                      rhs.astype(BF16))
            r_bar.append(view(r_t, it)[:, pl_] + ry[:, :LANES])
            y_bar.append(ry[:, LANES:])
            au.append(tx_p[:CHUNK] + tx_p[CHUNK:])
        v_q = view(v, it)
        mg_rhs = jnp.concatenate(
            [au[0], jnp.concatenate([z_half, v_q[:, :LANES]], axis=1),
             au[1], jnp.concatenate([z_half, v_q[:, LANES:]], axis=1)], axis=0)
        lhs_t = jnp.concatenate([view(b_h, it), view(k_h, it)], axis=0).T
        mg = _dot(stacked_bd(lhs_t).astype(BF16), mg_rhs.astype(BF16))
        m_sel = jnp.concatenate([mg[0:CHUNK], mg[2 * CHUNK:3 * CHUNK]], axis=1)
        m_odd = jnp.concatenate([mg[CHUNK:2 * CHUNK], mg[3 * CHUNK:]], axis=1)
        m_even_odd = jnp.where(jnp.concatenate([even_head, even_head], axis=1), m_sel, m_odd)
        m_quad = (jnp.concatenate([m_even_odd[:, :LANES], m_even_odd[:, 2 * LANES:3 * LANES]], axis=1)
                  + jnp.where(diag, view(gam_c, it), 0.0))
        g_quad = jnp.concatenate([m_even_odd[:, LANES:2 * LANES], m_even_odd[:, 3 * LANES:]], axis=1)
        cols = slice(qd * quad, (qd + 1) * quad)
        rm_ref[c, :, cols] = jnp.concatenate(
            [jnp.concatenate(r_bar, axis=1), m_quad], axis=0).astype(rm_ref.dtype)
        yg_ref[c, :, cols] = jnp.concatenate([jnp.concatenate(y_bar, axis=1), g_quad], axis=0)


def _rwkv_prep_kernel(rw_ref, prev_ref, mu_ref, w0_ref, a0_ref, kk_ref, ka_ref, rk_ref,
                      wa2_ref, g2_ref, ones_ref, tri_ref,
                      rm_ref, yg_ref, bonus_ref, g_ref, stage_a, stage_b,
                      *, ntiles, tiles_per_seq):
    t = pl.program_id(0)
    tile = jnp.minimum(t, ntiles - 1)
    seq_start = (tile % tiles_per_seq) == 0

    @pl.when(t == 0)
    def _():
        stage_b[...] = jnp.zeros(stage_b.shape, F32)

    def step(write_ref, read_ref):
        pending = [
            _rwkv_chain(read_ref, rm_ref, yg_ref),
            _rwkv_stage(seq_start, rw_ref, prev_ref, mu_ref, w0_ref, a0_ref, kk_ref, ka_ref,
                        rk_ref, wa2_ref, g2_ref, ones_ref, tri_ref, bonus_ref, g_ref, write_ref),
        ]
        while pending:
            for gen in list(pending):
                try:
                    next(gen)
                except StopIteration:
                    pending.remove(gen)

    pl.when(t % 2 == 0)(functools.partial(step, stage_a, stage_b))
    pl.when(t % 2 == 1)(functools.partial(step, stage_b, stage_a))


def _rwkv_prep(rw, mu, w0, a0, k_k, k_a, r_k, wa2, g2, ones_bd, tri, seq):
    m = rw.shape[0]
    tt = RW_PREP_ROWS
    cpt = tt // CHUNK
    ntiles = m // tt
    cur = lambda t: jnp.minimum(t, ntiles - 1)
    done = lambda t: jnp.maximum(t - 1, 0)
    vec = lambda n: pl.BlockSpec((1, n), lambda t: (0, 0))
    return pl.pallas_call(
        functools.partial(_rwkv_prep_kernel, ntiles=ntiles, tiles_per_seq=seq // tt),
        grid=(ntiles + 1,),
        in_specs=[
            pl.BlockSpec((tt, RW_IN_WIDTH), lambda t: (cur(t), 0)),
            pl.BlockSpec((8, RW_IN_WIDTH), lambda t: (jnp.maximum(cur(t) * (tt // 8) - 1, 0), 0)),
            vec(RW_IN_WIDTH), vec(RW_WIDTH), vec(RW_WIDTH), vec(RW_WIDTH), vec(RW_WIDTH),
            vec(RW_WIDTH),
            pl.BlockSpec((LANES, 2 * RW_WIDTH), lambda t: (0, 0)),
            pl.BlockSpec((RW_GATE_LORA, RW_WIDTH), lambda t: (0, 0)),
            pl.BlockSpec((SEG_TILE, SEG_TILE), lambda t: (0, 0)),
            pl.BlockSpec((tt, tt), lambda t: (0, 0)),
        ],
        out_specs=[
            pl.BlockSpec((cpt, 2 * CHUNK, RW_WIDTH), lambda t: (done(t), 0, 0)),
            pl.BlockSpec((cpt, 2 * CHUNK, RW_WIDTH), lambda t: (done(t), 0, 0)),
            pl.BlockSpec((tt, RW_WIDTH), lambda t: (cur(t), 0)),
            pl.BlockSpec((tt, RW_WIDTH), lambda t: (cur(t), 0)),
        ],
        out_shape=[
            jax.ShapeDtypeStruct((m // CHUNK, 2 * CHUNK, RW_WIDTH), BF16),
            jax.ShapeDtypeStruct((m // CHUNK, 2 * CHUNK, RW_WIDTH), F32),
            jax.ShapeDtypeStruct((m, RW_WIDTH), F32),
            jax.ShapeDtypeStruct((m, RW_WIDTH), F32),
        ],
        scratch_shapes=[
            pltpu.VMEM((len(_STAGED), tt, RW_WIDTH), F32),
            pltpu.VMEM((len(_STAGED), tt, RW_WIDTH), F32),
        ],
        compiler_params=pltpu.CompilerParams(
            dimension_semantics=("arbitrary",), vmem_limit_bytes=VMEM_LIMIT_BYTES),
        name="rwkv_prep",
    )(rw, rw, mu, w0, a0, k_k, k_a, r_k, wa2, g2, ones_bd, tri)


def _rwkv_scan_kernel(rm_ref, yg_ref, bonus_ref, g_ref, lng_ref, lnb_ref, ones_ref,
                      y_ref, h_sc, y_sc):
    step = pl.program_id(0)
    nb, nchunk = rm_ref.shape[0], rm_ref.shape[1]
    rows = nchunk * CHUNK

    @pl.when(step == 0)
    def _():
        h_sc[...] = jnp.zeros(h_sc.shape, F32)

    r128 = lax.broadcasted_iota(jnp.int32, (2 * CHUNK, LANES), 0)
    l128 = lax.broadcasted_iota(jnp.int32, (2 * CHUNK, LANES), 1)
    bd_mask = (r128 < CHUNK) == (l128 < RW_HEAD)

    for c in range(nchunk):
        for bi in range(nb):
            for p in range(RW_PAIRS):
                lanes = slice(p * LANES, (p + 1) * LANES)
                h_bd = h_sc[bi * RW_PAIRS + p]
                out = _dot(rm_ref[bi, c, :, lanes], h_bd.astype(BF16)) + yg_ref[bi, c, :, lanes]
                y_sc[bi, c * CHUNK:(c + 1) * CHUNK, lanes] = out[:CHUNK]
                h_new = out[CHUNK:]
                h_sc[bi * RW_PAIRS + p] = jnp.where(
                    bd_mask, jnp.concatenate([h_new, h_new], axis=0), 0.0)

    ones_bd = ones_ref[...]
    y = y_sc[...].reshape(nb * rows, RW_WIDTH)
    mean = _segsum(y, ones_bd) * (1.0 / RW_HEAD)
    d = y - mean
    var = _segsum(d * d, ones_bd) * (1.0 / RW_HEAD)
    yn = d * lax.rsqrt(var + RW_GN_EPS) * lng_ref[...] + lnb_ref[...]
    bonus = bonus_ref[...].reshape(nb * rows, RW_WIDTH)
    gate = g_ref[...].reshape(nb * rows, RW_WIDTH)
    y_ref[...] = ((yn + bonus) * gate).reshape(nb, rows, RW_WIDTH).astype(y_ref.dtype)


def _rwkv_scan(rm, yg, bonus, g, ln_g, ln_b, ones_bd):
    b, nc, _, _ = rm.shape
    s = nc * CHUNK
    vec = lambda n: pl.BlockSpec((1, n), lambda ci: (0, 0))
    cps = SCAN_CHUNKS
    rows = cps * CHUNK
    assert nc % cps == 0
    return pl.pallas_call(
        _rwkv_scan_kernel,
        grid=(nc // cps,),
        in_specs=[
            pl.BlockSpec((b, cps, 2 * CHUNK, RW_WIDTH), lambda ci: (0, ci, 0, 0)),
            pl.BlockSpec((b, cps, 2 * CHUNK, RW_WIDTH), lambda ci: (0, ci, 0, 0)),
            pl.BlockSpec((b, rows, RW_WIDTH), lambda ci: (0, ci, 0)),
            pl.BlockSpec((b, rows, RW_WIDTH), lambda ci: (0, ci, 0)),
            vec(RW_WIDTH), vec(RW_WIDTH),
            pl.BlockSpec((SEG_TILE, SEG_TILE), lambda ci: (0, 0)),
        ],
        out_specs=pl.BlockSpec((b, rows, RW_WIDTH), lambda ci: (0, ci, 0)),
        out_shape=jax.ShapeDtypeStruct((b, s, RW_WIDTH), BF16),
        scratch_shapes=[
            pltpu.VMEM((b * RW_PAIRS, 2 * CHUNK, LANES), F32),
            pltpu.VMEM((b, rows, RW_WIDTH), F32),
        ],
        compiler_params=pltpu.CompilerParams(
            dimension_semantics=("arbitrary",), vmem_limit_bytes=VMEM_LIMIT_BYTES),
        name="rwkv_scan",
    )(rm, yg, bonus, g, ln_g, ln_b, ones_bd)


def _merge_ffn_kernel(x_ref, ya_ref, yb_ref, gate_ref, wa_ref, wb_ref, wo_ref,
                      gffn_ref, w1_ref, w2_ref, gfin_ref, o_ref):
    pa = _dot(ya_ref[...], wa_ref[...])
    pb = _dot(yb_ref[...], wb_ref[...])
    gate = gate_ref[...]
    merged = (jax.nn.sigmoid(gate[:, :D_MODEL]) * pa
              + jax.nn.sigmoid(gate[:, D_MODEL:]) * pb)
    x1 = x_ref[...] + _dot(merged.astype(BF16), wo_ref[...])
    ms = jnp.mean(x1 * x1, axis=-1, keepdims=True)
    h = (x1 * lax.rsqrt(ms + NORM_EPS) * gffn_ref[...]).astype(BF16)
    acc = x1
    for c in range(D_FF // FFN_CHUNK):
        cols = slice(c * FFN_CHUNK, (c + 1) * FFN_CHUNK)
        f = jnp.maximum(_dot(h, w1_ref[:, cols]), 0.0)
        acc = acc + _dot((f * f).astype(BF16), w2_ref[cols, :])
    ms2 = jnp.mean(acc * acc, axis=-1, keepdims=True)
    o_ref[...] = acc * lax.rsqrt(ms2 + NORM_EPS) * gfin_ref[...]


def _merge_ffn(x2d, ya, yb, gates, wa, wb, wo, gffn, w1, w2, gfin):
    m = x2d.shape[0]
    tm = FFN_ROWS
    row = lambda i: (i, 0)
    return pl.pallas_call(
        _merge_ffn_kernel,
        grid=(m // tm,),
        in_specs=[
            pl.BlockSpec((tm, D_MODEL), row),
            pl.BlockSpec((tm, DA_WIDTH), row),
            pl.BlockSpec((tm, RW_WIDTH), row),
            pl.BlockSpec((tm, GATE_WIDTH), row),
            _resident((DA_WIDTH, D_MODEL)),
            _resident((RW_WIDTH, D_MODEL)),
            _resident((D_MODEL, D_MODEL)),
            _resident((1, D_MODEL)),
            _resident((D_MODEL, D_FF)),
            _resident((D_FF, D_MODEL)),
            _resident((1, D_MODEL)),
        ],
        out_specs=pl.BlockSpec((tm, D_MODEL), row),
        out_shape=jax.ShapeDtypeStruct((m, D_MODEL), F32),
        compiler_params=pltpu.CompilerParams(
            dimension_semantics=("parallel",), vmem_limit_bytes=VMEM_LIMIT_BYTES),
        name="merge_ffn",
    )(x2d, ya, yb, gates, wa, wb, wo, gffn, w1, w2, gfin)


def _rope_tables(seq):
    d = DA_HALF_DIM
    pos = jnp.arange(seq, dtype=F32)
    inv_freq = ROPE_THETA ** (-jnp.arange(0, d, 2, dtype=F32) / d)
    ang = pos[:, None] * inv_freq[None, :]
    cos = jnp.cos(ang)
    sin = jnp.sin(ang)
    reps = LANES // d
    cos_full = jnp.tile(jnp.concatenate([cos, cos], axis=-1), (1, reps))
    sin_full = jnp.tile(jnp.concatenate([-sin, sin], axis=-1), (1, reps))
    return cos_full, sin_full


def kernel(x, norm_mix_g, w_in, rw_mu, rw_w0, rw_w2, rw_a0, rw_a2, rw_g2, rw_k_k, rw_k_a,
           rw_r_k, rw_ln_g, rw_ln_b, da_lq1, da_lk1, da_lq2, da_lk2, da_subln_g,
           w_branch_a, w_branch_b, w_o, norm_ffn_g, w_ff1, w_ff2, norm_final_g):
    b, s, d = x.shape
    assert d == D_MODEL and norm_mix_g.shape[0] == 1
    assert s % ATTN_BLOCK == 0 and s % INPROJ_ROWS == 0 and s % RW_PREP_ROWS == 0
    assert (b * s) % FFN_ROWS == 0
    row = lambda t: t.reshape(1, -1)
    x2d = x.reshape(b * s, d)

    cos, sin = _rope_tables(s)
    q, k, v, rw, gates = _inproj(x2d, norm_mix_g, w_in[0].astype(BF16), cos, sin, s)

    ya = _attention(q.reshape(b, s, DA_WIDTH), k.reshape(b, s, DA_WIDTH),
                    v.reshape(b, s, DA_WIDTH), row(da_lq1), row(da_lk1), row(da_lq2),
                    row(da_lk2), row(da_subln_g))

    wa2 = jnp.zeros((LANES, 2 * RW_WIDTH), F32)
    wa2 = wa2.at[:RW_DECAY_LORA, :RW_WIDTH].set(rw_w2[0])
    wa2 = wa2.at[RW_DECAY_LORA:, RW_WIDTH:].set(rw_a2[0])
    idx = jnp.arange(SEG_TILE) // RW_HEAD
    ones_bd = (idx[:, None] == idx[None, :]).astype(BF16)
    tok = jnp.arange(RW_PREP_ROWS)
    tri = ((tok[:, None] >= tok[None, :])
           & (tok[:, None] // CHUNK == tok[None, :] // CHUNK)).astype(BF16)

    rm, yg, bonus, g = _rwkv_prep(
        rw, rw_mu, rw_w0, rw_a0, rw_k_k, rw_k_a, row(rw_r_k),
        wa2.astype(BF16), rw_g2[0].astype(BF16), ones_bd, tri, s)
    nc = s // CHUNK
    yb = _rwkv_scan(rm.reshape(b, nc, 2 * CHUNK, RW_WIDTH), yg.reshape(b, nc, 2 * CHUNK, RW_WIDTH),
                    bonus.reshape(b, s, RW_WIDTH), g.reshape(b, s, RW_WIDTH),
                    rw_ln_g, rw_ln_b, ones_bd)

    out = _merge_ffn(x2d, ya.reshape(b * s, DA_WIDTH), yb.reshape(b * s, RW_WIDTH), gates,
                     w_branch_a[0].astype(BF16), w_branch_b[0].astype(BF16),
                     w_o[0].astype(BF16), norm_ffn_g, w_ff1[0].astype(BF16),
                     w_ff2[0].astype(BF16), row(norm_final_g))
    return out.reshape(b, s, d)
```

```python
import functools
import math

import jax
import jax.numpy as jnp
from jax import lax
from jax.experimental import pallas as pl
from jax.experimental.pallas import tpu as pltpu

F32 = jnp.float32
BF16 = jnp.bfloat16

D_MODEL = 1024
CHUNK = 64
NORM_EPS = 1e-6
ROPE_THETA = 10000.0
DA_WIDTH = 512
DA_HEADS = 4
DA_HALF_DIM = 64
DA_VDIM = 128
DA_SUBLN_EPS = 1e-5
RW_WIDTH = 512
RW_HEAD = 64
RW_PAIRS = RW_WIDTH // (2 * RW_HEAD)
RW_DECAY_LORA = 64
RW_AAA_LORA = 64
RW_GATE_LORA = 128
RW_GN_EPS = RW_HEAD * 1e-5
RW_IN_WIDTH = 3 * RW_WIDTH + RW_DECAY_LORA + RW_AAA_LORA + RW_GATE_LORA
DA_IN_WIDTH = 3 * DA_WIDTH
GATE_WIDTH = 2 * D_MODEL
D_IN = DA_IN_WIDTH + RW_IN_WIDTH + GATE_WIDTH
D_FF = 4 * D_MODEL
LAMBDA_INIT = 0.8 - 0.6 * math.exp(0.0)
Q_SCALE = DA_HALF_DIM ** -0.5 * math.log2(math.e)

LANES = 128
MXU_TILE = 256
SEG_TILE = MXU_TILE
VMEM_LIMIT_BYTES = 56 * 1024 * 1024

INPROJ_ROWS = 512
ATTN_BLOCK = 256
ATTN_BATCH_GROUP = 1
RW_PREP_ROWS = 512
SCAN_CHUNKS = 2
FFN_ROWS = 512
FFN_CHUNK = 1024


def _dot(a, b):
    return jnp.dot(a, b, preferred_element_type=F32)


def _dot_nt(a, b):
    return lax.dot_general(a, b, (((1,), (1,)), ((), ())), preferred_element_type=F32)


def _split2(x):
    hi = x.astype(BF16)
    lo = (x - hi.astype(F32)).astype(BF16)
    return hi, lo


def _segsum(x, ones_bd):
    hi, lo = _split2(x)
    w = ones_bd.shape[0]
    parts = [_dot(hi[:, c:c + w], ones_bd) + _dot(lo[:, c:c + w], ones_bd)
             for c in range(0, x.shape[1], w)]
    return jnp.concatenate(parts, axis=1)


def _resident(shape):
    nd = len(shape)
    return pl.BlockSpec(shape, lambda *_: (0,) * nd, pipeline_mode=pl.Buffered(1))


def _inproj_kernel(x_ref, g_ref, w_ref, cos_ref, sin_ref,
                   q_ref, k_ref, v_ref, rw_ref, gate_ref):
    x = x_ref[...]
    ms = jnp.mean(x * x, axis=-1, keepdims=True)
    h = (x * lax.rsqrt(ms + NORM_EPS) * g_ref[...]).astype(BF16)

    def proj(c0, c1):
        return _dot(h, w_ref[:, c0:c1])

    rows = x.shape[0]
    lane = lax.broadcasted_iota(jnp.int32, (rows, DA_WIDTH), 1)
    first_half = (lane & (DA_HALF_DIM - 1)) < (DA_HALF_DIM // 2)

    reps = DA_WIDTH // LANES
    cos = jnp.concatenate([cos_ref[...]] * reps, axis=1)
    sin = jnp.concatenate([sin_ref[...]] * reps, axis=1)

    def rope(t):
        partner = jnp.where(first_half,
                            pltpu.roll(t, DA_WIDTH - DA_HALF_DIM // 2, 1),
                            pltpu.roll(t, DA_HALF_DIM // 2, 1))
        return t * cos + partner * sin

    q_ref[...] = (rope(proj(0, DA_WIDTH)) * Q_SCALE).astype(q_ref.dtype)
    k_ref[...] = rope(proj(DA_WIDTH, 2 * DA_WIDTH)).astype(k_ref.dtype)
    v_ref[...] = proj(2 * DA_WIDTH, 3 * DA_WIDTH).astype(v_ref.dtype)
    rw_ref[...] = proj(DA_IN_WIDTH, DA_IN_WIDTH + RW_IN_WIDTH)
    gate_ref[...] = proj(DA_IN_WIDTH + RW_IN_WIDTH, D_IN)


def _inproj(x2d, g, w_bf, cos, sin, seq):
    m = x2d.shape[0]
    tm = INPROJ_ROWS
    nseq = seq // tm
    row = lambda i: (i, 0)
    tab = lambda i: (i % nseq, 0)
    return pl.pallas_call(
        _inproj_kernel,
        grid=(m // tm,),
        in_specs=[
            pl.BlockSpec((tm, D_MODEL), row),
            _resident((1, D_MODEL)),
            _resident((D_MODEL, D_IN)),
            pl.BlockSpec((tm, LANES), tab),
            pl.BlockSpec((tm, LANES), tab),
        ],
        out_specs=[
            pl.BlockSpec((tm, DA_WIDTH), row),
            pl.BlockSpec((tm, DA_WIDTH), row),
            pl.BlockSpec((tm, DA_WIDTH), row),
            pl.BlockSpec((tm, RW_IN_WIDTH), row),
            pl.BlockSpec((tm, GATE_WIDTH), row),
        ],
        out_shape=[
            jax.ShapeDtypeStruct((m, DA_WIDTH), BF16),
            jax.ShapeDtypeStruct((m, DA_WIDTH), BF16),
            jax.ShapeDtypeStruct((m, DA_WIDTH), BF16),
            jax.ShapeDtypeStruct((m, RW_IN_WIDTH), F32),
            jax.ShapeDtypeStruct((m, GATE_WIDTH), F32),
        ],
        compiler_params=pltpu.CompilerParams(
            dimension_semantics=("parallel",), vmem_limit_bytes=VMEM_LIMIT_BYTES),
        name="inproj",
    )(x2d, g, w_bf, cos, sin)


def _attn_kernel(lq1_ref, lk1_ref, lq2_ref, lk2_ref, sg_ref, q_ref, k_ref, v_ref,
                 o_ref):
    nbatch, tq, width = q_ref.shape
    heads = width // DA_VDIM
    i = pl.program_id(1)
    lane = lax.broadcasted_iota(jnp.int32, (tq, DA_VDIM), 1)
    lam = (jnp.exp(jnp.sum(lq1_ref[...] * lk1_ref[...], axis=-1, keepdims=True))
           - jnp.exp(jnp.sum(lq2_ref[...] * lk2_ref[...], axis=-1, keepdims=True))
           + LAMBDA_INIT)
    r = lax.broadcasted_iota(jnp.int32, (2 * tq, tq), 0)
    c = lax.broadcasted_iota(jnp.int32, (2 * tq, tq), 1)
    allowed = (c // CHUNK) <= ((r & (tq - 1)) // CHUNK)

    def attend(nfull, bi, h):
        cols = slice(h * DA_VDIM, (h + 1) * DA_VDIM)
        q = q_ref[bi, :, cols]
        zero = jnp.zeros_like(q)
        qq = jnp.concatenate([jnp.where(lane < DA_HALF_DIM, q, zero),
                              jnp.where(lane >= DA_HALF_DIM, q, zero)], axis=0)
        s = []
        for j in range(nfull + 1):
            s_j = _dot_nt(qq, k_ref[bi, j * tq:(j + 1) * tq, cols])
            s.append(jnp.where(allowed, s_j, -jnp.inf) if j == nfull else s_j)
            yield
        m_lane = s[0]
        for s_j in s[1:]:
            m_lane = jnp.maximum(m_lane, s_j)
        m = jnp.max(m_lane, axis=-1, keepdims=True)
        l_lane = None
        acc = None
        for j, s_j in enumerate(s):
            p = jnp.exp2(s_j - m)
            pv = _dot(p.astype(BF16), v_ref[bi, j * tq:(j + 1) * tq, cols])
            l_lane = p if l_lane is None else l_lane + p
            acc = pv if acc is None else acc + pv
            if j < nfull:
                yield
        l = jnp.sum(l_lane, axis=-1, keepdims=True)
        o = acc / l
        od = o[:tq] - lam * o[tq:]
        ms = jnp.mean(od * od, axis=-1, keepdims=True)
        y = od * lax.rsqrt(ms + DA_SUBLN_EPS) * sg_ref[...]
        o_ref[bi, :, cols] = (y * (1.0 - LAMBDA_INIT)).astype(o_ref.dtype)

    def variant(nfull):
        gens = [attend(nfull, bi, h) for bi in range(nbatch) for h in range(heads)]
        nscore = nfull + 1
        for _ in range(nscore):
            next(gens[0])
        for k, cur in enumerate(gens):
            ahead = gens[k + 1] if k + 1 < len(gens) else None
            left = nscore if ahead is not None else 0
            for _ in cur:
                if left:
                    next(ahead)
                    left -= 1
            for _ in range(left):
                next(ahead)

    for n in range(k_ref.shape[1] // tq):
        pl.when(i == n)(functools.partial(variant, n))


def _attention(q, k, v, lq1, lk1, lq2, lk2, subln_g):
    b, s, _ = q.shape
    tq = ATTN_BLOCK
    assert tq & (tq - 1) == 0 and s % tq == 0
    vec = lambda n: pl.BlockSpec((1, n), lambda bi, qi: (0, 0))
    nb = ATTN_BATCH_GROUP
    assert b % nb == 0
    return pl.pallas_call(
        _attn_kernel,
        grid=(b // nb, s // tq),
        in_specs=[
            vec(DA_HALF_DIM), vec(DA_HALF_DIM), vec(DA_HALF_DIM), vec(DA_HALF_DIM),
            vec(DA_VDIM),
            pl.BlockSpec((nb, tq, DA_WIDTH), lambda bi, qi: (bi, qi, 0)),
            pl.BlockSpec((nb, s, DA_WIDTH), lambda bi, qi: (bi, 0, 0)),
            pl.BlockSpec((nb, s, DA_WIDTH), lambda bi, qi: (bi, 0, 0)),
        ],
        out_specs=pl.BlockSpec((nb, tq, DA_WIDTH), lambda bi, qi: (bi, qi, 0)),
        out_shape=jax.ShapeDtypeStruct((b, s, DA_WIDTH), BF16),
        compiler_params=pltpu.CompilerParams(
            dimension_semantics=("parallel", "arbitrary"),
            vmem_limit_bytes=VMEM_LIMIT_BYTES),
        name="diff_attn",
    )(lq1, lk1, lq2, lk2, subln_g, q, k, v)


_STAGED = ("r_t", "a_t", "k_t", "b_t", "k_h", "b_h", "v", "gam_c")


def _rwkv_stage(seq_start, rw_ref, prev_ref, mu_ref, w0_ref, a0_ref, kk_ref, ka_ref, rk_ref,
                wa2_ref, g2_ref, ones_ref, tri_ref, bonus_ref, g_ref, stage_ref):
    tt = rw_ref.shape[0]
    z = rw_ref[...]
    prev = jnp.where(seq_start, 0.0, prev_ref[7:8, :])
    row = lax.broadcasted_iota(jnp.int32, z.shape, 0)
    z_prev = jnp.where(row == 0, prev, pltpu.roll(z, 1, 0))
    zs = z + (z_prev - z) * mu_ref[...]
    yield

    w3 = 3 * RW_WIDTH
    r = zs[:, 0:RW_WIDTH]
    k = zs[:, RW_WIDTH:2 * RW_WIDTH]
    v = zs[:, 2 * RW_WIDTH:w3]
    x_wa = zs[:, w3:w3 + LANES]
    x_g = zs[:, w3 + LANES:w3 + 2 * LANES]
    lane = lax.broadcasted_iota(jnp.int32, x_wa.shape, 1)
    t_wa = jnp.where(lane < RW_DECAY_LORA, jnp.tanh(x_wa), x_wa)
    lora = _dot(t_wa.astype(BF16), wa2_ref[...])
    u = -(w0_ref[...] + lora[:, :RW_WIDTH])
    softplus = jnp.maximum(u, 0.0) + jnp.log(1.0 + jnp.exp(-jnp.abs(u)))
    log_decay = -jnp.exp(-softplus - 0.5)
    yield
    a_lr = jax.nn.sigmoid(a0_ref[...] + lora[:, RW_WIDTH:])
    g_ref[...] = _dot(jax.nn.sigmoid(x_g).astype(BF16), g2_ref[...])
    yield

    ones_bd = ones_ref[...]
    kk = k * kk_ref[...]
    kk = kk * jnp.minimum(lax.rsqrt(_segsum(kk * kk, ones_bd)), 1e12)
    yield
    k2 = k * (1.0 + (a_lr - 1.0) * ka_ref[...])
    a_vec = -kk
    b_vec = kk * a_lr
    bonus_ref[...] = _segsum(r * k2 * rk_ref[...], ones_bd) * v
    yield

    tri = tri_ref[...]
    h1 = log_decay.astype(BF16)
    r1 = log_decay - h1.astype(F32)
    h2 = r1.astype(BF16)
    h3 = (r1 - h2.astype(F32)).astype(BF16)
    tw = tri.shape[0]
    cs = jnp.concatenate(
        [_dot(tri, h1[r0:r0 + tw]) + _dot(tri, h2[r0:r0 + tw]) + _dot(tri, h3[r0:r0 + tw])
         for r0 in range(0, tt, tw)], axis=0)
    nchunk = tt // CHUNK
    cl = jnp.concatenate(
        [jnp.broadcast_to(cs[(c + 1) * CHUNK - 1:(c + 1) * CHUNK, :], (CHUNK, RW_WIDTH))
         for c in range(nchunk)], axis=0)
    yield
    stage_ref[_STAGED.index("v")] = v
    stage_ref[_STAGED.index("gam_c")] = jnp.exp(cl)
    stage_ref[_STAGED.index("r_t")] = r * jnp.exp(cs)
    yield
    stage_ref[_STAGED.index("a_t")] = a_vec * jnp.exp(cs - log_decay)
    yield
    gam_inv = jnp.exp(-cs)
    stage_ref[_STAGED.index("k_t")] = k2 * gam_inv
    stage_ref[_STAGED.index("b_t")] = b_vec * gam_inv
    yield
    gam_end = jnp.exp(cl - cs)
    stage_ref[_STAGED.index("k_h")] = k2 * gam_end
    stage_ref[_STAGED.index("b_h")] = b_vec * gam_end


def _rwkv_chain(stage_ref, rm_ref, yg_ref):
    r_t, a_t, k_t, b_t, k_h, b_h, v, gam_c = _STAGED
    tt = stage_ref.shape[1]
    nchunk = tt // CHUNK

    quad = 2 * LANES
    cq = (CHUNK, quad)
    t_idx = lax.broadcasted_iota(jnp.int32, cq, 0)
    l_idx = lax.broadcasted_iota(jnp.int32, cq, 1)
    s_idx = l_idx & (RW_HEAD - 1)
    strict = s_idx < t_idx
    incl = s_idx <= t_idx
    diag = s_idx == t_idx
    even_head = (l_idx & RW_HEAD) == 0
    r4 = lax.broadcasted_iota(jnp.int32, (quad, quad), 0)
    l4 = lax.broadcasted_iota(jnp.int32, (quad, quad), 1)
    bd4_mask = (r4 // RW_HEAD) == (l4 // RW_HEAD)
    r2 = lax.broadcasted_iota(jnp.int32, (2 * CHUNK, LANES), 0)
    l2 = lax.broadcasted_iota(jnp.int32, (2 * CHUNK, LANES), 1)
    bd_mask = (r2 < CHUNK) == (l2 < RW_HEAD)
    eye = (r2 == l2).astype(F32)
    eye2 = jnp.concatenate([eye, eye], axis=0)
    z_pair = jnp.zeros((2 * CHUNK, LANES), F32)
    z_half = jnp.zeros((CHUNK, LANES), F32)

    def bd(x):
        return jnp.where(bd_mask, jnp.concatenate([x, x], axis=0), 0.0)

    def bd4(x):
        return jnp.where(bd4_mask, jnp.concatenate([x, x, x, x], axis=0), 0.0)

    def blockdiag2(xa, xb):
        zero = z_pair.astype(xa.dtype)
        return jnp.concatenate([jnp.concatenate([xa, zero], axis=1),
                                jnp.concatenate([zero, xb], axis=1)], axis=0)

    def stacked_bd(st):
        return blockdiag2(st[:2 * CHUNK], st[2 * CHUNK:])

    items = [(c, qd) for c in range(nchunk) for qd in range(RW_WIDTH // quad)]

    def view(name, item):
        c, qd = item
        return stage_ref[_STAGED.index(name), c * CHUNK:(c + 1) * CHUNK, qd * quad:(qd + 1) * quad]

    a2 = [_dot_nt(jnp.concatenate([view(a_t, it), view(r_t, it)], axis=0).astype(BF16),
                  jnp.concatenate([bd4(view(b_t, it)), bd4(view(k_t, it))], axis=0).astype(BF16))
          for it in items]
    yield
    a_ab = [jnp.where(strict, z[:CHUNK, :quad], 0.0) for z in a2]
    a_ak = [jnp.where(strict, z[:CHUNK, quad:], 0.0) for z in a2]
    a_rb = [jnp.where(incl, z[CHUNK:, :quad], 0.0) for z in a2]
    a_rk = [jnp.where(incl, z[CHUNK:, quad:], 0.0) for z in a2]

    n_st = [jnp.concatenate([bd(z[:, :LANES]), bd(z[:, LANES:])], axis=0) for z in a_ab]
    t_st = [eye2 + n for n in n_st]
    n_bf = [n.astype(BF16) for n in n_st]
    p_st = [_dot(stacked_bd(n), n) for n in n_bf]
    yield
    for _ in range(4):
        p_bf = [p.astype(BF16) for p in p_st]
        zz = [_dot(stacked_bd(p), jnp.concatenate([p, t.astype(BF16)], axis=1))
              for p, t in zip(p_bf, t_st)]
        p_st = [z[:, :LANES] for z in zz]
        t_st = [t + z[:, LANES:] for t, z in zip(t_st, zz)]
        yield
    t_st = [t + _dot(stacked_bd(p.astype(BF16)), t.astype(BF16)) for p, t in zip(p_st, t_st)]
    yield

    bd4_v = [bd4(view(v, it)) for it in items]
    av = [_dot(x.astype(BF16), bv.astype(BF16)) for x, bv in zip(a_ak, bd4_v)]
    x_st = [jnp.concatenate(
        [jnp.concatenate([bd(view(a_t, it)[:, :LANES]), bd(u[:, :LANES])], axis=1),
         jnp.concatenate([bd(view(a_t, it)[:, LANES:]), bd(u[:, LANES:])], axis=1)], axis=0)
        for it, u in zip(items, av)]
    tx = [_dot(stacked_bd(t).astype(BF16), x.astype(BF16)) for t, x in zip(t_st, x_st)]
    yield

    for n_item, (it, tx_i, arb, ark, bv4) in enumerate(zip(items, tx, a_rb, a_rk, bd4_v)):
        if n_item and n_item % 2 == 0:
            yield
        c, qd = it
        r_bar, y_bar, au = [], [], []
        for half in range(2):
            pl_ = slice(half * LANES, (half + 1) * LANES)
            rows = slice(half * 2 * CHUNK, (half + 1) * 2 * CHUNK)
            tx_p = tx_i[rows]
            rhs = jnp.concatenate(
                [tx_p, jnp.concatenate([z_pair, bv4[rows, pl_]], axis=1)], axis=0)
            ry = _dot(jnp.concatenate([arb[:, pl_], ark[:, pl_]], axis=1).astype(BF16),
                      rhs.astype(BF16))
            r_bar.append(view(r_t, it)[:, pl_] + ry[:, :LANES])
            y_bar.append(ry[:, LANES:])
            au.append(tx_p[:CHUNK] + tx_p[CHUNK:])
        v_q = view(v, it)
        mg_rhs = jnp.concatenate(
            [au[0], jnp.concatenate([z_half, v_q[:, :LANES]], axis=1),
             au[1], jnp.concatenate([z_half, v_q[:, LANES:]], axis=1)], axis=0)
        lhs_t = jnp.concatenate([view(b_h, it), view(k_h, it)], axis=0).T
        mg = _dot(stacked_bd(lhs_t).astype(BF16), mg_rhs.astype(BF16))
        m_sel = jnp.concatenate([mg[0:CHUNK], mg[2 * CHUNK:3 * CHUNK]], axis=1)
        m_odd = jnp.concatenate([mg[CHUNK:2 * CHUNK], mg[3 * CHUNK:]], axis=1)
        m_even_odd = jnp.where(jnp.concatenate([even_head, even_head], axis=1), m_sel, m_odd)
        m_quad = (jnp.concatenate([m_even_odd[:, :LANES], m_even_odd[:, 2 * LANES:3 * LANES]], axis=1)
                  + jnp.where(diag, view(gam_c, it), 0.0))
        g_quad = jnp.concatenate([m_even_odd[:, LANES:2 * LANES], m_even_odd[:, 3 * LANES:]], axis=1)
        cols = slice(qd * quad, (qd + 1) * quad)
        rm_ref[c, :, cols] = jnp.concatenate(
            [jnp.concatenate(r_bar, axis=1), m_quad], axis=0).astype(rm_ref.dtype)
        yg_ref[c, :, cols] = jnp.concatenate([jnp.concatenate(y_bar, axis=1), g_quad], axis=0)


def _rwkv_prep_kernel(rw_ref, prev_ref, mu_ref, w0_ref, a0_ref, kk_ref, ka_ref, rk_ref,
                      wa2_ref, g2_ref, ones_ref, tri_ref,
                      rm_ref, yg_ref, bonus_ref, g_ref, stage_a, stage_b,
                      *, ntiles, tiles_per_seq):
    t = pl.program_id(0)
    tile = jnp.minimum(t, ntiles - 1)
    seq_start = (tile % tiles_per_seq) == 0

    @pl.when(t == 0)
    def _():
        stage_b[...] = jnp.zeros(stage_b.shape, F32)

    def step(write_ref, read_ref):
        pending = [
            _rwkv_chain(read_ref, rm_ref, yg_ref),
            _rwkv_stage(seq_start, rw_ref, prev_ref, mu_ref, w0_ref, a0_ref, kk_ref, ka_ref,
                        rk_ref, wa2_ref, g2_ref, ones_ref, tri_ref, bonus_ref, g_ref, write_ref),
        ]
        while pending:
            for gen in list(pending):
                try:
                    next(gen)
                except StopIteration:
                    pending.remove(gen)

    pl.when(t % 2 == 0)(functools.partial(step, stage_a, stage_b))
    pl.when(t % 2 == 1)(functools.partial(step, stage_b, stage_a))


def _rwkv_prep(rw, mu, w0, a0, k_k, k_a, r_k, wa2, g2, ones_bd, tri, seq):
    m = rw.shape[0]
    tt = RW_PREP_ROWS
    cpt = tt // CHUNK
    ntiles = m // tt
    cur = lambda t: jnp.minimum(t, ntiles - 1)
    done = lambda t: jnp.maximum(t - 1, 0)
    vec = lambda n: pl.BlockSpec((1, n), lambda t: (0, 0))
    return pl.pallas_call(
        functools.partial(_rwkv_prep_kernel, ntiles=ntiles, tiles_per_seq=seq // tt),
        grid=(ntiles + 1,),
        in_specs=[
            pl.BlockSpec((tt, RW_IN_WIDTH), lambda t: (cur(t), 0)),
            pl.BlockSpec((8, RW_IN_WIDTH), lambda t: (jnp.maximum(cur(t) * (tt // 8) - 1, 0), 0)),
            vec(RW_IN_WIDTH), vec(RW_WIDTH), vec(RW_WIDTH), vec(RW_WIDTH), vec(RW_WIDTH),
            vec(RW_WIDTH),
            pl.BlockSpec((LANES, 2 * RW_WIDTH), lambda t: (0, 0)),
            pl.BlockSpec((RW_GATE_LORA, RW_WIDTH), lambda t: (0, 0)),
            pl.BlockSpec((SEG_TILE, SEG_TILE), lambda t: (0, 0)),
            pl.BlockSpec((MXU_TILE, MXU_TILE), lambda t: (0, 0)),
        ],
        out_specs=[
            pl.BlockSpec((cpt, 2 * CHUNK, RW_WIDTH), lambda t: (done(t), 0, 0)),
            pl.BlockSpec((cpt, 2 * CHUNK, RW_WIDTH), lambda t: (done(t), 0, 0)),
            pl.BlockSpec((tt, RW_WIDTH), lambda t: (cur(t), 0)),
            pl.BlockSpec((tt, RW_WIDTH), lambda t: (cur(t), 0)),
        ],
        out_shape=[
            jax.ShapeDtypeStruct((m // CHUNK, 2 * CHUNK, RW_WIDTH), BF16),
            jax.ShapeDtypeStruct((m // CHUNK, 2 * CHUNK, RW_WIDTH), F32),
            jax.ShapeDtypeStruct((m, RW_WIDTH), F32),
            jax.ShapeDtypeStruct((m, RW_WIDTH), F32),
        ],
        scratch_shapes=[
            pltpu.VMEM((len(_STAGED), tt, RW_WIDTH), F32),
            pltpu.VMEM((len(_STAGED), tt, RW_WIDTH), F32),
        ],
        compiler_params=pltpu.CompilerParams(
            dimension_semantics=("arbitrary",), vmem_limit_bytes=VMEM_LIMIT_BYTES),
        name="rwkv_prep",
    )(rw, rw, mu, w0, a0, k_k, k_a, r_k, wa2, g2, ones_bd, tri)


def _rwkv_scan_kernel(rm_ref, yg_ref, bonus_ref, g_ref, lng_ref, lnb_ref, ones_ref,
                      y_ref, h_sc, y_sc):
    step = pl.program_id(0)
    nb, nchunk = rm_ref.shape[0], rm_ref.shape[1]
    rows = nchunk * CHUNK

    @pl.when(step == 0)
    def _():
        h_sc[...] = jnp.zeros(h_sc.shape, F32)

    r128 = lax.broadcasted_iota(jnp.int32, (2 * CHUNK, LANES), 0)
    l128 = lax.broadcasted_iota(jnp.int32, (2 * CHUNK, LANES), 1)
    bd_mask = (r128 < CHUNK) == (l128 < RW_HEAD)

    for c in range(nchunk):
        for bi in range(nb):
            for p in range(RW_PAIRS):
                lanes = slice(p * LANES, (p + 1) * LANES)
                h_bd = h_sc[bi * RW_PAIRS + p]
                out = _dot(rm_ref[bi, c, :, lanes], h_bd.astype(BF16)) + yg_ref[bi, c, :, lanes]
                y_sc[bi, c * CHUNK:(c + 1) * CHUNK, lanes] = out[:CHUNK]
                h_new = out[CHUNK:]
                h_sc[bi * RW_PAIRS + p] = jnp.where(
                    bd_mask, jnp.concatenate([h_new, h_new], axis=0), 0.0)

    ones_bd = ones_ref[...]
    y = y_sc[...].reshape(nb * rows, RW_WIDTH)
    mean = _segsum(y, ones_bd) * (1.0 / RW_HEAD)
    d = y - mean
    var = _segsum(d * d, ones_bd) * (1.0 / RW_HEAD)
    yn = d * lax.rsqrt(var + RW_GN_EPS) * lng_ref[...] + lnb_ref[...]
    bonus = bonus_ref[...].reshape(nb * rows, RW_WIDTH)
    gate = g_ref[...].reshape(nb * rows, RW_WIDTH)
    y_ref[...] = ((yn + bonus) * gate).reshape(nb, rows, RW_WIDTH).astype(y_ref.dtype)


def _rwkv_scan(rm, yg, bonus, g, ln_g, ln_b, ones_bd):
    b, nc, _, _ = rm.shape
    s = nc * CHUNK
    vec = lambda n: pl.BlockSpec((1, n), lambda ci: (0, 0))
    cps = SCAN_CHUNKS
    rows = cps * CHUNK
    assert nc % cps == 0
    return pl.pallas_call(
        _rwkv_scan_kernel,
        grid=(nc // cps,),
        in_specs=[
            pl.BlockSpec((b, cps, 2 * CHUNK, RW_WIDTH), lambda ci: (0, ci, 0, 0)),
            pl.BlockSpec((b, cps, 2 * CHUNK, RW_WIDTH), lambda ci: (0, ci, 0, 0)),
            pl.BlockSpec((b, rows, RW_WIDTH), lambda ci: (0, ci, 0)),
            pl.BlockSpec((b, rows, RW_WIDTH), lambda ci: (0, ci, 0)),
            vec(RW_WIDTH), vec(RW_WIDTH),
            pl.BlockSpec((SEG_TILE, SEG_TILE), lambda ci: (0, 0)),
        ],
        out_specs=pl.BlockSpec((b, rows, RW_WIDTH), lambda ci: (0, ci, 0)),
        out_shape=jax.ShapeDtypeStruct((b, s, RW_WIDTH), BF16),
        scratch_shapes=[
            pltpu.VMEM((b * RW_PAIRS, 2 * CHUNK, LANES), F32),
            pltpu.VMEM((b, rows, RW_WIDTH), F32),
        ],
        compiler_params=pltpu.CompilerParams(
            dimension_semantics=("arbitrary",), vmem_limit_bytes=VMEM_LIMIT_BYTES),
        name="rwkv_scan",
    )(rm, yg, bonus, g, ln_g, ln_b, ones_bd)


def _merge_ffn_kernel(x_ref, ya_ref, yb_ref, gate_ref, wa_ref, wb_ref, wo_ref,
                      gffn_ref, w1_ref, w2_ref, gfin_ref, o_ref):
    pa = _dot(ya_ref[...], wa_ref[...])
    pb = _dot(yb_ref[...], wb_ref[...])
    gate = gate_ref[...]
    merged = (jax.nn.sigmoid(gate[:, :D_MODEL]) * pa
              + jax.nn.sigmoid(gate[:, D_MODEL:]) * pb)
    x1 = x_ref[...] + _dot(merged.astype(BF16), wo_ref[...])
    ms = jnp.mean(x1 * x1, axis=-1, keepdims=True)
    h = (x1 * lax.rsqrt(ms + NORM_EPS) * gffn_ref[...]).astype(BF16)
    acc = x1
    for c in range(D_FF // FFN_CHUNK):
        cols = slice(c * FFN_CHUNK, (c + 1) * FFN_CHUNK)
        f = jnp.maximum(_dot(h, w1_ref[:, cols]), 0.0)
        acc = acc + _dot((f * f).astype(BF16), w2_ref[cols, :])
    ms2 = jnp.mean(acc * acc, axis=-1, keepdims=True)
    o_ref[...] = acc * lax.rsqrt(ms2 + NORM_EPS) * gfin_ref[...]


def _merge_ffn(x2d, ya, yb, gates, wa, wb, wo, gffn, w1, w2, gfin):
    m = x2d.shape[0]
    tm = FFN_ROWS
    row = lambda i: (i, 0)
    return pl.pallas_call(
        _merge_ffn_kernel,
        grid=(m // tm,),
        in_specs=[
            pl.BlockSpec((tm, D_MODEL), row),
            pl.BlockSpec((tm, DA_WIDTH), row),
            pl.BlockSpec((tm, RW_WIDTH), row),
            pl.BlockSpec((tm, GATE_WIDTH), row),
            _resident((DA_WIDTH, D_MODEL)),
            _resident((RW_WIDTH, D_MODEL)),
            _resident((D_MODEL, D_MODEL)),
            _resident((1, D_MODEL)),
            _resident((D_MODEL, D_FF)),
            _resident((D_FF, D_MODEL)),
            _resident((1, D_MODEL)),
        ],
        out_specs=pl.BlockSpec((tm, D_MODEL), row),
        out_shape=jax.ShapeDtypeStruct((m, D_MODEL), F32),
        compiler_params=pltpu.CompilerParams(
            dimension_semantics=("parallel",), vmem_limit_bytes=VMEM_LIMIT_BYTES),
        name="merge_ffn",
    )(x2d, ya, yb, gates, wa, wb, wo, gffn, w1, w2, gfin)


def _rope_tables(seq):
    d = DA_HALF_DIM
    pos = jnp.arange(seq, dtype=F32)
    inv_freq = ROPE_THETA ** (-jnp.arange(0, d, 2, dtype=F32) / d)
    ang = pos[:, None] * inv_freq[None, :]
    cos = jnp.cos(ang)
    sin = jnp.sin(ang)
    reps = LANES // d
    cos_full = jnp.tile(jnp.concatenate([cos, cos], axis=-1), (1, reps))
    sin_full = jnp.tile(jnp.concatenate([-sin, sin], axis=-1), (1, reps))
    return cos_full, sin_full


def kernel(x, norm_mix_g, w_in, rw_mu, rw_w0, rw_w2, rw_a0, rw_a2, rw_g2, rw_k_k, rw_k_a,
           rw_r_k, rw_ln_g, rw_ln_b, da_lq1, da_lk1, da_lq2, da_lk2, da_subln_g,
           w_branch_a, w_branch_b, w_o, norm_ffn_g, w_ff1, w_ff2, norm_final_g):
    b, s, d = x.shape
    assert d == D_MODEL and norm_mix_g.shape[0] == 1
    assert s % ATTN_BLOCK == 0 and s % INPROJ_ROWS == 0 and s % RW_PREP_ROWS == 0
    assert (b * s) % FFN_ROWS == 0
    row = lambda t: t.reshape(1, -1)
    x2d = x.reshape(b * s, d)

    cos, sin = _rope_tables(s)
    q, k, v, rw, gates = _inproj(x2d, norm_mix_g, w_in[0].astype(BF16), cos, sin, s)

    ya = _attention(q.reshape(b, s, DA_WIDTH), k.reshape(b, s, DA_WIDTH),
                    v.reshape(b, s, DA_WIDTH), row(da_lq1), row(da_lk1), row(da_lq2),
                    row(da_lk2), row(da_subln_g))

    wa2 = jnp.zeros((LANES, 2 * RW_WIDTH), F32)
    wa2 = wa2.at[:RW_DECAY_LORA, :RW_WIDTH].set(rw_w2[0])
    wa2 = wa2.at[RW_DECAY_LORA:, RW_WIDTH:].set(rw_a2[0])
    idx = jnp.arange(SEG_TILE) // RW_HEAD
    ones_bd = (idx[:, None] == idx[None, :]).astype(BF16)
    tok = jnp.arange(MXU_TILE)
    tri = ((tok[:, None] >= tok[None, :])
           & (tok[:, None] // CHUNK == tok[None, :] // CHUNK)).astype(BF16)

    rm, yg, bonus, g = _rwkv_prep(
        rw, rw_mu, rw_w0, rw_a0, rw_k_k, rw_k_a, row(rw_r_k),
        wa2.astype(BF16), rw_g2[0].astype(BF16), ones_bd, tri, s)
    nc = s // CHUNK
    yb = _rwkv_scan(rm.reshape(b, nc, 2 * CHUNK, RW_WIDTH), yg.reshape(b, nc, 2 * CHUNK, RW_WIDTH),
                    bonus.reshape(b, s, RW_WIDTH), g.reshape(b, s, RW_WIDTH),
                    rw_ln_g, rw_ln_b, ones_bd)

    out = _merge_ffn(x2d, ya.reshape(b * s, DA_WIDTH), yb.reshape(b * s, RW_WIDTH), gates,
                     w_branch_a[0].astype(BF16), w_branch_b[0].astype(BF16),
                     w_o[0].astype(BF16), norm_ffn_g, w_ff1[0].astype(BF16),
                     w_ff2[0].astype(BF16), row(norm_final_g))
    return out.reshape(b, s, d)
```

```python
import functools
import math

import jax
import jax.numpy as jnp
from jax import lax
from jax.experimental import pallas as pl
from jax.experimental.pallas import tpu as pltpu

F32 = jnp.float32
BF16 = jnp.bfloat16

D_MODEL = 1024
CHUNK = 64
NORM_EPS = 1e-6
ROPE_THETA = 10000.0
DA_WIDTH = 512
DA_HEADS = 4
DA_HALF_DIM = 64
DA_VDIM = 128
DA_SUBLN_EPS = 1e-5
RW_WIDTH = 512
RW_HEAD = 64
RW_PAIRS = RW_WIDTH // (2 * RW_HEAD)
RW_DECAY_LORA = 64
RW_AAA_LORA = 64
RW_GATE_LORA = 128
RW_GN_EPS = RW_HEAD * 1e-5
RW_IN_WIDTH = 3 * RW_WIDTH + RW_DECAY_LORA + RW_AAA_LORA + RW_GATE_LORA
DA_IN_WIDTH = 3 * DA_WIDTH
GATE_WIDTH = 2 * D_MODEL
D_IN = DA_IN_WIDTH + RW_IN_WIDTH + GATE_WIDTH
D_FF = 4 * D_MODEL
LAMBDA_INIT = 0.8 - 0.6 * math.exp(0.0)
Q_SCALE = DA_HALF_DIM ** -0.5 * math.log2(math.e)

LANES = 128
MXU_TILE = 256
SEG_TILE = MXU_TILE
VMEM_LIMIT_BYTES = 56 * 1024 * 1024

INPROJ_ROWS = 512
ATTN_BLOCK = 256
ATTN_BATCH_GROUP = 1
RW_PREP_ROWS = 512
SCAN_CHUNKS = 4
FFN_ROWS = 512
FFN_CHUNK = 1024


def _dot(a, b):
    return jnp.dot(a, b, preferred_element_type=F32)


def _dot_nt(a, b):
    return lax.dot_general(a, b, (((1,), (1,)), ((), ())), preferred_element_type=F32)


def _split2(x):
    hi = x.astype(BF16)
    lo = (x - hi.astype(F32)).astype(BF16)
    return hi, lo


def _segsum(x, ones_bd):
    hi, lo = _split2(x)
    w = ones_bd.shape[0]
    parts = [_dot(hi[:, c:c + w], ones_bd) + _dot(lo[:, c:c + w], ones_bd)
             for c in range(0, x.shape[1], w)]
    return jnp.concatenate(parts, axis=1)


def _resident(shape):
    nd = len(shape)
    return pl.BlockSpec(shape, lambda *_: (0,) * nd, pipeline_mode=pl.Buffered(1))


def _inproj_kernel(x_ref, g_ref, w_ref, cos_ref, sin_ref,
                   q_ref, k_ref, v_ref, rw_ref, gate_ref):
    x = x_ref[...]
    ms = jnp.mean(x * x, axis=-1, keepdims=True)
    h = (x * lax.rsqrt(ms + NORM_EPS) * g_ref[...]).astype(BF16)

    def proj(c0, c1):
        return _dot(h, w_ref[:, c0:c1])

    rows = x.shape[0]
    lane = lax.broadcasted_iota(jnp.int32, (rows, DA_WIDTH), 1)
    first_half = (lane & (DA_HALF_DIM - 1)) < (DA_HALF_DIM // 2)

    reps = DA_WIDTH // LANES
    cos = jnp.concatenate([cos_ref[...]] * reps, axis=1)
    sin = jnp.concatenate([sin_ref[...]] * reps, axis=1)

    def rope(t):
        partner = jnp.where(first_half,
                            pltpu.roll(t, DA_WIDTH - DA_HALF_DIM // 2, 1),
                            pltpu.roll(t, DA_HALF_DIM // 2, 1))
        return t * cos + partner * sin

    q_ref[...] = (rope(proj(0, DA_WIDTH)) * Q_SCALE).astype(q_ref.dtype)
    k_ref[...] = rope(proj(DA_WIDTH, 2 * DA_WIDTH)).astype(k_ref.dtype)
    v_ref[...] = proj(2 * DA_WIDTH, 3 * DA_WIDTH).astype(v_ref.dtype)
    rw_ref[...] = proj(DA_IN_WIDTH, DA_IN_WIDTH + RW_IN_WIDTH)
    gate_ref[...] = proj(DA_IN_WIDTH + RW_IN_WIDTH, D_IN)


def _inproj(x2d, g, w_bf, cos, sin, seq):
    m = x2d.shape[0]
    tm = INPROJ_ROWS
    nseq = seq // tm
    row = lambda i: (i, 0)
    tab = lambda i: (i % nseq, 0)
    return pl.pallas_call(
        _inproj_kernel,
        grid=(m // tm,),
        in_specs=[
            pl.BlockSpec((tm, D_MODEL), row),
            _resident((1, D_MODEL)),
            _resident((D_MODEL, D_IN)),
            pl.BlockSpec((tm, LANES), tab),
            pl.BlockSpec((tm, LANES), tab),
        ],
        out_specs=[
            pl.BlockSpec((tm, DA_WIDTH), row),
            pl.BlockSpec((tm, DA_WIDTH), row),
            pl.BlockSpec((tm, DA_WIDTH), row),
            pl.BlockSpec((tm, RW_IN_WIDTH), row),
            pl.BlockSpec((tm, GATE_WIDTH), row),
        ],
        out_shape=[
            jax.ShapeDtypeStruct((m, DA_WIDTH), BF16),
            jax.ShapeDtypeStruct((m, DA_WIDTH), BF16),
            jax.ShapeDtypeStruct((m, DA_WIDTH), BF16),
            jax.ShapeDtypeStruct((m, RW_IN_WIDTH), F32),
            jax.ShapeDtypeStruct((m, GATE_WIDTH), F32),
        ],
        compiler_params=pltpu.CompilerParams(
            dimension_semantics=("parallel",), vmem_limit_bytes=VMEM_LIMIT_BYTES),
        name="inproj",
    )(x2d, g, w_bf, cos, sin)


def _attn_kernel(lq1_ref, lk1_ref, lq2_ref, lk2_ref, sg_ref, q_ref, k_ref, v_ref,
                 o_ref):
    nbatch, tq, width = q_ref.shape
    heads = width // DA_VDIM
    i = pl.program_id(1)
    lane = lax.broadcasted_iota(jnp.int32, (tq, DA_VDIM), 1)
    lam = (jnp.exp(jnp.sum(lq1_ref[...] * lk1_ref[...], axis=-1, keepdims=True))
           - jnp.exp(jnp.sum(lq2_ref[...] * lk2_ref[...], axis=-1, keepdims=True))
           + LAMBDA_INIT)
    r = lax.broadcasted_iota(jnp.int32, (2 * tq, tq), 0)
    c = lax.broadcasted_iota(jnp.int32, (2 * tq, tq), 1)
    allowed = (c // CHUNK) <= ((r & (tq - 1)) // CHUNK)

    def attend(nfull, bi, h):
        cols = slice(h * DA_VDIM, (h + 1) * DA_VDIM)
        q = q_ref[bi, :, cols]
        zero = jnp.zeros_like(q)
        qq = jnp.concatenate([jnp.where(lane < DA_HALF_DIM, q, zero),
                              jnp.where(lane >= DA_HALF_DIM, q, zero)], axis=0)
        s = []
        for j in range(nfull + 1):
            s_j = _dot_nt(qq, k_ref[bi, j * tq:(j + 1) * tq, cols])
            s.append(jnp.where(allowed, s_j, -jnp.inf) if j == nfull else s_j)
            yield
        m_lane = s[0]
        for s_j in s[1:]:
            m_lane = jnp.maximum(m_lane, s_j)
        m = jnp.max(m_lane, axis=-1, keepdims=True)
        l_lane = None
        acc = None
        for j, s_j in enumerate(s):
            p = jnp.exp2(s_j - m)
            pv = _dot(p.astype(BF16), v_ref[bi, j * tq:(j + 1) * tq, cols])
            l_lane = p if l_lane is None else l_lane + p
            acc = pv if acc is None else acc + pv
            if j < nfull:
                yield
        l = jnp.sum(l_lane, axis=-1, keepdims=True)
        o = acc / l
        od = o[:tq] - lam * o[tq:]
        ms = jnp.mean(od * od, axis=-1, keepdims=True)
        y = od * lax.rsqrt(ms + DA_SUBLN_EPS) * sg_ref[...]
        o_ref[bi, :, cols] = (y * (1.0 - LAMBDA_INIT)).astype(o_ref.dtype)

    def variant(nfull):
        gens = [attend(nfull, bi, h) for bi in range(nbatch) for h in range(heads)]
        nscore = nfull + 1
        for _ in range(nscore):
            next(gens[0])
        for k, cur in enumerate(gens):
            ahead = gens[k + 1] if k + 1 < len(gens) else None
            left = nscore if ahead is not None else 0
            for _ in cur:
                if left:
                    next(ahead)
                    left -= 1
            for _ in range(left):
                next(ahead)

    for n in range(k_ref.shape[1] // tq):
        pl.when(i == n)(functools.partial(variant, n))


def _attention(q, k, v, lq1, lk1, lq2, lk2, subln_g):
    b, s, _ = q.shape
    tq = ATTN_BLOCK
    assert tq & (tq - 1) == 0 and s % tq == 0
    vec = lambda n: pl.BlockSpec((1, n), lambda bi, qi: (0, 0))
    nb = ATTN_BATCH_GROUP
    assert b % nb == 0
    return pl.pallas_call(
        _attn_kernel,
        grid=(b // nb, s // tq),
        in_specs=[
            vec(DA_HALF_DIM), vec(DA_HALF_DIM), vec(DA_HALF_DIM), vec(DA_HALF_DIM),
            vec(DA_VDIM),
            pl.BlockSpec((nb, tq, DA_WIDTH), lambda bi, qi: (bi, qi, 0)),
            pl.BlockSpec((nb, s, DA_WIDTH), lambda bi, qi: (bi, 0, 0)),
            pl.BlockSpec((nb, s, DA_WIDTH), lambda bi, qi: (bi, 0, 0)),
        ],
        out_specs=pl.BlockSpec((nb, tq, DA_WIDTH), lambda bi, qi: (bi, qi, 0)),
        out_shape=jax.ShapeDtypeStruct((b, s, DA_WIDTH), BF16),
        compiler_params=pltpu.CompilerParams(
            dimension_semantics=("parallel", "arbitrary"),
            vmem_limit_bytes=VMEM_LIMIT_BYTES),
        name="diff_attn",
    )(lq1, lk1, lq2, lk2, subln_g, q, k, v)


_STAGED = ("r_t", "a_t", "k_t", "b_t", "k_h", "b_h", "v", "gam_c")


def _rwkv_stage(seq_start, rw_ref, prev_ref, mu_ref, w0_ref, a0_ref, kk_ref, ka_ref, rk_ref,
                wa2_ref, g2_ref, ones_ref, tri_ref, bonus_ref, g_ref, stage_ref):
    tt = rw_ref.shape[0]
    z = rw_ref[...]
    prev = jnp.where(seq_start, 0.0, prev_ref[7:8, :])
    row = lax.broadcasted_iota(jnp.int32, z.shape, 0)
    z_prev = jnp.where(row == 0, prev, pltpu.roll(z, 1, 0))
    zs = z + (z_prev - z) * mu_ref[...]
    yield

    w3 = 3 * RW_WIDTH
    r = zs[:, 0:RW_WIDTH]
    k = zs[:, RW_WIDTH:2 * RW_WIDTH]
    v = zs[:, 2 * RW_WIDTH:w3]
    x_wa = zs[:, w3:w3 + LANES]
    x_g = zs[:, w3 + LANES:w3 + 2 * LANES]
    lane = lax.broadcasted_iota(jnp.int32, x_wa.shape, 1)
    t_wa = jnp.where(lane < RW_DECAY_LORA, jnp.tanh(x_wa), x_wa)
    lora = _dot(t_wa.astype(BF16), wa2_ref[...])
    u = -(w0_ref[...] + lora[:, :RW_WIDTH])
    softplus = jnp.maximum(u, 0.0) + jnp.log(1.0 + jnp.exp(-jnp.abs(u)))
    log_decay = -jnp.exp(-softplus - 0.5)
    yield
    a_lr = jax.nn.sigmoid(a0_ref[...] + lora[:, RW_WIDTH:])
    g_ref[...] = _dot(jax.nn.sigmoid(x_g).astype(BF16), g2_ref[...]).astype(g_ref.dtype)
    yield

    ones_bd = ones_ref[...]
    kk = k * kk_ref[...]
    kk = kk * jnp.minimum(lax.rsqrt(_segsum(kk * kk, ones_bd)), 1e12)
    yield
    k2 = k * (1.0 + (a_lr - 1.0) * ka_ref[...])
    a_vec = -kk
    b_vec = kk * a_lr
    bonus_ref[...] = (_segsum(r * k2 * rk_ref[...], ones_bd) * v).astype(bonus_ref.dtype)
    yield

    tri = tri_ref[...]
    h1 = log_decay.astype(BF16)
    r1 = log_decay - h1.astype(F32)
    h2 = r1.astype(BF16)
    h3 = (r1 - h2.astype(F32)).astype(BF16)
    tw = tri.shape[0]
    cs = jnp.concatenate(
        [_dot(tri, h1[r0:r0 + tw]) + _dot(tri, h2[r0:r0 + tw]) + _dot(tri, h3[r0:r0 + tw])
         for r0 in range(0, tt, tw)], axis=0)
    nchunk = tt // CHUNK
    cl = jnp.concatenate(
        [jnp.broadcast_to(cs[(c + 1) * CHUNK - 1:(c + 1) * CHUNK, :], (CHUNK, RW_WIDTH))
         for c in range(nchunk)], axis=0)
    yield
    stage_ref[_STAGED.index("v")] = v
    stage_ref[_STAGED.index("gam_c")] = jnp.exp(cl)
    stage_ref[_STAGED.index("r_t")] = r * jnp.exp(cs)
    yield
    stage_ref[_STAGED.index("a_t")] = a_vec * jnp.exp(cs - log_decay)
    yield
    gam_inv = jnp.exp(-cs)
    stage_ref[_STAGED.index("k_t")] = k2 * gam_inv
    stage_ref[_STAGED.index("b_t")] = b_vec * gam_inv
    yield
    gam_end = jnp.exp(cl - cs)
    stage_ref[_STAGED.index("k_h")] = k2 * gam_end
    stage_ref[_STAGED.index("b_h")] = b_vec * gam_end


def _rwkv_chain(stage_ref, rm_ref, yg_ref):
    r_t, a_t, k_t, b_t, k_h, b_h, v, gam_c = _STAGED
    tt = stage_ref.shape[1]
    nchunk = tt // CHUNK

    quad = 2 * LANES
    cq = (CHUNK, quad)
    t_idx = lax.broadcasted_iota(jnp.int32, cq, 0)
    l_idx = lax.broadcasted_iota(jnp.int32, cq, 1)
    s_idx = l_idx & (RW_HEAD - 1)
    strict = s_idx < t_idx
    incl = s_idx <= t_idx
    diag = s_idx == t_idx
    even_head = (l_idx & RW_HEAD) == 0
    r4 = lax.broadcasted_iota(jnp.int32, (quad, quad), 0)
    l4 = lax.broadcasted_iota(jnp.int32, (quad, quad), 1)
    bd4_mask = (r4 // RW_HEAD) == (l4 // RW_HEAD)
    r2 = lax.broadcasted_iota(jnp.int32, (2 * CHUNK, LANES), 0)
    l2 = lax.broadcasted_iota(jnp.int32, (2 * CHUNK, LANES), 1)
    bd_mask = (r2 < CHUNK) == (l2 < RW_HEAD)
    eye = (r2 == l2).astype(F32)
    eye2 = jnp.concatenate([eye, eye], axis=0)
    z_pair = jnp.zeros((2 * CHUNK, LANES), F32)
    z_half = jnp.zeros((CHUNK, LANES), F32)

    def bd(x):
        return jnp.where(bd_mask, jnp.concatenate([x, x], axis=0), 0.0)

    def bd4(x):
        return jnp.where(bd4_mask, jnp.concatenate([x, x, x, x], axis=0), 0.0)

    def blockdiag2(xa, xb):
        zero = z_pair.astype(xa.dtype)
        return jnp.concatenate([jnp.concatenate([xa, zero], axis=1),
                                jnp.concatenate([zero, xb], axis=1)], axis=0)

    def stacked_bd(st):
        return blockdiag2(st[:2 * CHUNK], st[2 * CHUNK:])

    items = [(c, qd) for c in range(nchunk) for qd in range(RW_WIDTH // quad)]

    def view(name, item):
        c, qd = item
        return stage_ref[_STAGED.index(name), c * CHUNK:(c + 1) * CHUNK, qd * quad:(qd + 1) * quad]

    a2 = [_dot_nt(jnp.concatenate([view(a_t, it), view(r_t, it)], axis=0).astype(BF16),
                  jnp.concatenate([bd4(view(b_t, it)), bd4(view(k_t, it))], axis=0).astype(BF16))
          for it in items]
    yield
    a_ab = [jnp.where(strict, z[:CHUNK, :quad], 0.0) for z in a2]
    a_ak = [jnp.where(strict, z[:CHUNK, quad:], 0.0) for z in a2]
    a_rb = [jnp.where(incl, z[CHUNK:, :quad], 0.0) for z in a2]
    a_rk = [jnp.where(incl, z[CHUNK:, quad:], 0.0) for z in a2]

    n_st = [jnp.concatenate([bd(z[:, :LANES]), bd(z[:, LANES:])], axis=0) for z in a_ab]
    t_st = [eye2 + n for n in n_st]
    n_bf = [n.astype(BF16) for n in n_st]
    p_st = [_dot(stacked_bd(n), n) for n in n_bf]
    yield
    for _ in range(4):
        p_bf = [p.astype(BF16) for p in p_st]
        zz = [_dot(stacked_bd(p), jnp.concatenate([p, t.astype(BF16)], axis=1))
              for p, t in zip(p_bf, t_st)]
        p_st = [z[:, :LANES] for z in zz]
        t_st = [t + z[:, LANES:] for t, z in zip(t_st, zz)]
        yield
    t_st = [t + _dot(stacked_bd(p.astype(BF16)), t.astype(BF16)) for p, t in zip(p_st, t_st)]
    yield

    bd4_v = [bd4(view(v, it)) for it in items]
    av = [_dot(x.astype(BF16), bv.astype(BF16)) for x, bv in zip(a_ak, bd4_v)]
    x_st = [jnp.concatenate(
        [jnp.concatenate([bd(view(a_t, it)[:, :LANES]), bd(u[:, :LANES])], axis=1),
         jnp.concatenate([bd(view(a_t, it)[:, LANES:]), bd(u[:, LANES:])], axis=1)], axis=0)
        for it, u in zip(items, av)]
    tx = [_dot(stacked_bd(t).astype(BF16), x.astype(BF16)) for t, x in zip(t_st, x_st)]
    yield

    for n_item, (it, tx_i, arb, ark, bv4) in enumerate(zip(items, tx, a_rb, a_rk, bd4_v)):
        if n_item and n_item % 2 == 0:
            yield
        c, qd = it
        r_bar, y_bar, au = [], [], []
        for half in range(2):
            pl_ = slice(half * LANES, (half + 1) * LANES)
            rows = slice(half * 2 * CHUNK, (half + 1) * 2 * CHUNK)
            tx_p = tx_i[rows]
            rhs = jnp.concatenate(
                [tx_p, jnp.concatenate([z_pair, bv4[rows, pl_]], axis=1)], axis=0)
            ry = _dot(jnp.concatenate([arb[:, pl_], ark[:, pl_]], axis=1).astype(BF16),
                      rhs.astype(BF16))
            r_bar.append(view(r_t, it)[:, pl_] + ry[:, :LANES])
            y_bar.append(ry[:, LANES:])
            au.append(tx_p[:CHUNK] + tx_p[CHUNK:])
        v_q = view(v, it)
        mg_rhs = jnp.concatenate(
            [au[0], jnp.concatenate([z_half, v_q[:, :LANES]], axis=1),
             au[1], jnp.concatenate([z_half, v_q[:, LANES:]], axis=1)], axis=0)
        lhs_t = jnp.concatenate([view(b_h, it), view(k_h, it)], axis=0).T
        mg = _dot(stacked_bd(lhs_t).astype(BF16), mg_rhs.astype(BF16))
        m_sel = jnp.concatenate([mg[0:CHUNK], mg[2 * CHUNK:3 * CHUNK]], axis=1)
        m_odd = jnp.concatenate([mg[CHUNK:2 * CHUNK], mg[3 * CHUNK:]], axis=1)
        m_even_odd = jnp.where(jnp.concatenate([even_head, even_head], axis=1), m_sel, m_odd)
        m_quad = (jnp.concatenate([m_even_odd[:, :LANES], m_even_odd[:, 2 * LANES:3 * LANES]], axis=1)
                  + jnp.where(diag, view(gam_c, it), 0.0))
        g_quad = jnp.concatenate([m_even_odd[:, LANES:2 * LANES], m_even_odd[:, 3 * LANES:]], axis=1)
        cols = slice(qd * quad, (qd + 1) * quad)
        rm_ref[c, :, cols] = jnp.concatenate(
            [jnp.concatenate(r_bar, axis=1), m_quad], axis=0).astype(rm_ref.dtype)
        yg_ref[c, :, cols] = jnp.concatenate([jnp.concatenate(y_bar, axis=1), g_quad], axis=0)


def _rwkv_prep_kernel(rw_ref, prev_ref, mu_ref, w0_ref, a0_ref, kk_ref, ka_ref, rk_ref,
                      wa2_ref, g2_ref, ones_ref, tri_ref,
                      rm_ref, yg_ref, bonus_ref, g_ref, stage_a, stage_b,
                      *, ntiles, tiles_per_seq):
    t = pl.program_id(0)
    tile = jnp.minimum(t, ntiles - 1)
    seq_start = (tile % tiles_per_seq) == 0

    @pl.when(t == 0)
    def _():
        stage_b[...] = jnp.zeros(stage_b.shape, F32)

    def step(write_ref, read_ref):
        pending = [
            _rwkv_chain(read_ref, rm_ref, yg_ref),
            _rwkv_stage(seq_start, rw_ref, prev_ref, mu_ref, w0_ref, a0_ref, kk_ref, ka_ref,
                        rk_ref, wa2_ref, g2_ref, ones_ref, tri_ref, bonus_ref, g_ref, write_ref),
        ]
        while pending:
            for gen in list(pending):
                try:
                    next(gen)
                except StopIteration:
                    pending.remove(gen)

    pl.when(t % 2 == 0)(functools.partial(step, stage_a, stage_b))
    pl.when(t % 2 == 1)(functools.partial(step, stage_b, stage_a))


def _rwkv_prep(rw, mu, w0, a0, k_k, k_a, r_k, wa2, g2, ones_bd, tri, seq):
    m = rw.shape[0]
    tt = RW_PREP_ROWS
    cpt = tt // CHUNK
    ntiles = m // tt
    cur = lambda t: jnp.minimum(t, ntiles - 1)
    done = lambda t: jnp.maximum(t - 1, 0)
    vec = lambda n: pl.BlockSpec((1, n), lambda t: (0, 0))
    return pl.pallas_call(
        functools.partial(_rwkv_prep_kernel, ntiles=ntiles, tiles_per_seq=seq // tt),
        grid=(ntiles + 1,),
        in_specs=[
            pl.BlockSpec((tt, RW_IN_WIDTH), lambda t: (cur(t), 0)),
            pl.BlockSpec((8, RW_IN_WIDTH), lambda t: (jnp.maximum(cur(t) * (tt // 8) - 1, 0), 0)),
            vec(RW_IN_WIDTH), vec(RW_WIDTH), vec(RW_WIDTH), vec(RW_WIDTH), vec(RW_WIDTH),
            vec(RW_WIDTH),
            pl.BlockSpec((LANES, 2 * RW_WIDTH), lambda t: (0, 0)),
            pl.BlockSpec((RW_GATE_LORA, RW_WIDTH), lambda t: (0, 0)),
            pl.BlockSpec((SEG_TILE, SEG_TILE), lambda t: (0, 0)),
            pl.BlockSpec((MXU_TILE, MXU_TILE), lambda t: (0, 0)),
        ],
        out_specs=[
            pl.BlockSpec((cpt, 2 * CHUNK, RW_WIDTH), lambda t: (done(t), 0, 0)),
            pl.BlockSpec((cpt, 2 * CHUNK, RW_WIDTH), lambda t: (done(t), 0, 0)),
            pl.BlockSpec((tt, RW_WIDTH), lambda t: (cur(t), 0)),
            pl.BlockSpec((tt, RW_WIDTH), lambda t: (cur(t), 0)),
        ],
        out_shape=[
            jax.ShapeDtypeStruct((m // CHUNK, 2 * CHUNK, RW_WIDTH), BF16),
            jax.ShapeDtypeStruct((m // CHUNK, 2 * CHUNK, RW_WIDTH), F32),
            jax.ShapeDtypeStruct((m, RW_WIDTH), BF16),
            jax.ShapeDtypeStruct((m, RW_WIDTH), BF16),
        ],
        scratch_shapes=[
            pltpu.VMEM((len(_STAGED), tt, RW_WIDTH), F32),
            pltpu.VMEM((len(_STAGED), tt, RW_WIDTH), F32),
        ],
        compiler_params=pltpu.CompilerParams(
            dimension_semantics=("arbitrary",), vmem_limit_bytes=VMEM_LIMIT_BYTES),
        name="rwkv_prep",
    )(rw, rw, mu, w0, a0, k_k, k_a, r_k, wa2, g2, ones_bd, tri)


def _rwkv_scan_kernel(rm_ref, yg_ref, bonus_ref, g_ref, lng_ref, lnb_ref, ones_ref,
                      y_ref, h_sc, y_sc):
    step = pl.program_id(0)
    nb, nchunk = rm_ref.shape[0], rm_ref.shape[1]
    rows = nchunk * CHUNK

    @pl.when(step == 0)
    def _():
        h_sc[...] = jnp.zeros(h_sc.shape, F32)

    r128 = lax.broadcasted_iota(jnp.int32, (2 * CHUNK, LANES), 0)
    l128 = lax.broadcasted_iota(jnp.int32, (2 * CHUNK, LANES), 1)
    bd_mask = (r128 < CHUNK) == (l128 < RW_HEAD)

    for c in range(nchunk):
        for bi in range(nb):
            for p in range(RW_PAIRS):
                lanes = slice(p * LANES, (p + 1) * LANES)
                h_bd = h_sc[bi * RW_PAIRS + p]
                out = _dot(rm_ref[bi, c, :, lanes], h_bd.astype(BF16)) + yg_ref[bi, c, :, lanes]
                y_sc[bi, c * CHUNK:(c + 1) * CHUNK, lanes] = out[:CHUNK]
                h_new = out[CHUNK:]
                h_sc[bi * RW_PAIRS + p] = jnp.where(
                    bd_mask, jnp.concatenate([h_new, h_new], axis=0), 0.0)

    ones_bd = ones_ref[...]
    y = y_sc[...].reshape(nb * rows, RW_WIDTH)
    mean = _segsum(y, ones_bd) * (1.0 / RW_HEAD)
    d = y - mean
    var = _segsum(d * d, ones_bd) * (1.0 / RW_HEAD)
    yn = d * lax.rsqrt(var + RW_GN_EPS) * lng_ref[...] + lnb_ref[...]
    bonus = bonus_ref[...].astype(F32).reshape(nb * rows, RW_WIDTH)
    gate = g_ref[...].astype(F32).reshape(nb * rows, RW_WIDTH)
    y_ref[...] = ((yn + bonus) * gate).reshape(nb, rows, RW_WIDTH).astype(y_ref.dtype)


def _rwkv_scan(rm, yg, bonus, g, ln_g, ln_b, ones_bd):
    b, nc, _, _ = rm.shape
    s = nc * CHUNK
    vec = lambda n: pl.BlockSpec((1, n), lambda ci: (0, 0))
    cps = SCAN_CHUNKS
    rows = cps * CHUNK
    assert nc % cps == 0
    return pl.pallas_call(
        _rwkv_scan_kernel,
        grid=(nc // cps,),
        in_specs=[
            pl.BlockSpec((b, cps, 2 * CHUNK, RW_WIDTH), lambda ci: (0, ci, 0, 0)),
            pl.BlockSpec((b, cps, 2 * CHUNK, RW_WIDTH), lambda ci: (0, ci, 0, 0)),
            pl.BlockSpec((b, rows, RW_WIDTH), lambda ci: (0, ci, 0)),
            pl.BlockSpec((b, rows, RW_WIDTH), lambda ci: (0, ci, 0)),
            vec(RW_WIDTH), vec(RW_WIDTH),
            pl.BlockSpec((SEG_TILE, SEG_TILE), lambda ci: (0, 0)),
        ],
        out_specs=pl.BlockSpec((b, rows, RW_WIDTH), lambda ci: (0, ci, 0)),
        out_shape=jax.ShapeDtypeStruct((b, s, RW_WIDTH), BF16),
        scratch_shapes=[
            pltpu.VMEM((b * RW_PAIRS, 2 * CHUNK, LANES), F32),
            pltpu.VMEM((b, rows, RW_WIDTH), F32),
        ],
        compiler_params=pltpu.CompilerParams(
            dimension_semantics=("arbitrary",), vmem_limit_bytes=VMEM_LIMIT_BYTES),
        name="rwkv_scan",
    )(rm, yg, bonus, g, ln_g, ln_b, ones_bd)


def _merge_ffn_kernel(x_ref, ya_ref, yb_ref, gate_ref, wa_ref, wb_ref, wo_ref,
                      gffn_ref, w1_ref, w2_ref, gfin_ref, o_ref):
    pa = _dot(ya_ref[...], wa_ref[...])
    pb = _dot(yb_ref[...], wb_ref[...])
    gate = gate_ref[...]
    merged = (jax.nn.sigmoid(gate[:, :D_MODEL]) * pa
              + jax.nn.sigmoid(gate[:, D_MODEL:]) * pb)
    x1 = x_ref[...] + _dot(merged.astype(BF16), wo_ref[...])
    ms = jnp.mean(x1 * x1, axis=-1, keepdims=True)
    h = (x1 * lax.rsqrt(ms + NORM_EPS) * gffn_ref[...]).astype(BF16)
    acc = x1
    for c in range(D_FF // FFN_CHUNK):
        cols = slice(c * FFN_CHUNK, (c + 1) * FFN_CHUNK)
        f = jnp.maximum(_dot(h, w1_ref[:, cols]), 0.0)
        acc = acc + _dot((f * f).astype(BF16), w2_ref[cols, :])
    ms2 = jnp.mean(acc * acc, axis=-1, keepdims=True)
    o_ref[...] = acc * lax.rsqrt(ms2 + NORM_EPS) * gfin_ref[...]


def _merge_ffn(x2d, ya, yb, gates, wa, wb, wo, gffn, w1, w2, gfin):
    m = x2d.shape[0]
    tm = FFN_ROWS
    row = lambda i: (i, 0)
    return pl.pallas_call(
        _merge_ffn_kernel,
        grid=(m // tm,),
        in_specs=[
            pl.BlockSpec((tm, D_MODEL), row),
            pl.BlockSpec((tm, DA_WIDTH), row),
            pl.BlockSpec((tm, RW_WIDTH), row),
            pl.BlockSpec((tm, GATE_WIDTH), row),
            _resident((DA_WIDTH, D_MODEL)),
            _resident((RW_WIDTH, D_MODEL)),
            _resident((D_MODEL, D_MODEL)),
            _resident((1, D_MODEL)),
            _resident((D_MODEL, D_FF)),
            _resident((D_FF, D_MODEL)),
            _resident((1, D_MODEL)),
        ],
        out_specs=pl.BlockSpec((tm, D_MODEL), row),
        out_shape=jax.ShapeDtypeStruct((m, D_MODEL), F32),
        compiler_params=pltpu.CompilerParams(
            dimension_semantics=("parallel",), vmem_limit_bytes=VMEM_LIMIT_BYTES),
        name="merge_ffn",
    )(x2d, ya, yb, gates, wa, wb, wo, gffn, w1, w2, gfin)


def _rope_tables(seq):
    d = DA_HALF_DIM
    pos = jnp.arange(seq, dtype=F32)
    inv_freq = ROPE_THETA ** (-jnp.arange(0, d, 2, dtype=F32) / d)
    ang = pos[:, None] * inv_freq[None, :]
    cos = jnp.cos(ang)
    sin = jnp.sin(ang)
    reps = LANES // d
    cos_full = jnp.tile(jnp.concatenate([cos, cos], axis=-1), (1, reps))
    sin_full = jnp.tile(jnp.concatenate([-sin, sin], axis=-1), (1, reps))
    return cos_full, sin_full


def kernel(x, norm_mix_g, w_in, rw_mu, rw_w0, rw_w2, rw_a0, rw_a2, rw_g2, rw_k_k, rw_k_a,
           rw_r_k, rw_ln_g, rw_ln_b, da_lq1, da_lk1, da_lq2, da_lk2, da_subln_g,
           w_branch_a, w_branch_b, w_o, norm_ffn_g, w_ff1, w_ff2, norm_final_g):
    b, s, d = x.shape
    assert d == D_MODEL and norm_mix_g.shape[0] == 1
    assert s % ATTN_BLOCK == 0 and s % INPROJ_ROWS == 0 and s % RW_PREP_ROWS == 0
    assert (b * s) % FFN_ROWS == 0
    row = lambda t: t.reshape(1, -1)
    x2d = x.reshape(b * s, d)

    cos, sin = _rope_tables(s)
    q, k, v, rw, gates = _inproj(x2d, norm_mix_g, w_in[0].astype(BF16), cos, sin, s)

    ya = _attention(q.reshape(b, s, DA_WIDTH), k.reshape(b, s, DA_WIDTH),
                    v.reshape(b, s, DA_WIDTH), row(da_lq1), row(da_lk1), row(da_lq2),
                    row(da_lk2), row(da_subln_g))

    wa2 = jnp.zeros((LANES, 2 * RW_WIDTH), F32)
    wa2 = wa2.at[:RW_DECAY_LORA, :RW_WIDTH].set(rw_w2[0])
    wa2 = wa2.at[RW_DECAY_LORA:, RW_WIDTH:].set(rw_a2[0])
    idx = jnp.arange(SEG_TILE) // RW_HEAD
    ones_bd = (idx[:, None] == idx[None, :]).astype(BF16)
    tok = jnp.arange(MXU_TILE)
    tri = ((tok[:, None] >= tok[None, :])
           & (tok[:, None] // CHUNK == tok[None, :] // CHUNK)).astype(BF16)

    rm, yg, bonus, g = _rwkv_prep(
        rw, rw_mu, rw_w0, rw_a0, rw_k_k, rw_k_a, row(rw_r_k),
        wa2.astype(BF16), rw_g2[0].astype(BF16), ones_bd, tri, s)
    nc = s // CHUNK
    yb = _rwkv_scan(rm.reshape(b, nc, 2 * CHUNK, RW_WIDTH), yg.reshape(b, nc, 2 * CHUNK, RW_WIDTH),
                    bonus.reshape(b, s, RW_WIDTH), g.reshape(b, s, RW_WIDTH),
                    rw_ln_g, rw_ln_b, ones_bd)

    out = _merge_ffn(x2d, ya.reshape(b * s, DA_WIDTH), yb.reshape(b * s, RW_WIDTH), gates,
                     w_branch_a[0].astype(BF16), w_branch_b[0].astype(BF16),
                     w_o[0].astype(BF16), norm_ffn_g, w_ff1[0].astype(BF16),
                     w_ff2[0].astype(BF16), row(norm_final_g))
    return out.reshape(b, s, d)
```

```python
import functools
import math

import jax
import jax.numpy as jnp
from jax import lax
from jax.experimental import pallas as pl
from jax.experimental.pallas import tpu as pltpu

F32 = jnp.float32
BF16 = jnp.bfloat16

D_MODEL = 1024
CHUNK = 64
NORM_EPS = 1e-6
ROPE_THETA = 10000.0
DA_WIDTH = 512
DA_HEADS = 4
DA_HALF_DIM = 64
DA_VDIM = 128
DA_SUBLN_EPS = 1e-5
RW_WIDTH = 512
RW_HEAD = 64
RW_PAIRS = RW_WIDTH // (2 * RW_HEAD)
RW_DECAY_LORA = 64
RW_AAA_LORA = 64
RW_GATE_LORA = 128
RW_GN_EPS = RW_HEAD * 1e-5
RW_IN_WIDTH = 3 * RW_WIDTH + RW_DECAY_LORA + RW_AAA_LORA + RW_GATE_LORA
DA_IN_WIDTH = 3 * DA_WIDTH
GATE_WIDTH = 2 * D_MODEL
D_IN = DA_IN_WIDTH + RW_IN_WIDTH + GATE_WIDTH
D_FF = 4 * D_MODEL
LAMBDA_INIT = 0.8 - 0.6 * math.exp(0.0)
Q_SCALE = DA_HALF_DIM ** -0.5 * math.log2(math.e)

LANES = 128
MXU_TILE = 256
SEG_TILE = MXU_TILE
VMEM_LIMIT_BYTES = 56 * 1024 * 1024

INPROJ_ROWS = 512
ATTN_BLOCK = 256
ATTN_BATCH_GROUP = 1
RW_PREP_ROWS = 512
SCAN_CHUNKS = 4
FFN_ROWS = 512
FFN_CHUNK = 1024


def _dot(a, b):
    return jnp.dot(a, b, preferred_element_type=F32)


def _dot_nt(a, b):
    return lax.dot_general(a, b, (((1,), (1,)), ((), ())), preferred_element_type=F32)


def _split2(x):
    hi = x.astype(BF16)
    lo = (x - hi.astype(F32)).astype(BF16)
    return hi, lo


def _segsum(x, ones_bd):
    hi, lo = _split2(x)
    w = ones_bd.shape[0]
    parts = [_dot(hi[:, c:c + w], ones_bd) + _dot(lo[:, c:c + w], ones_bd)
             for c in range(0, x.shape[1], w)]
    return jnp.concatenate(parts, axis=1)


def _resident(shape):
    nd = len(shape)
    return pl.BlockSpec(shape, lambda *_: (0,) * nd, pipeline_mode=pl.Buffered(1))


def _inproj_kernel(x_ref, g_ref, w_ref, cos_ref, sin_ref,
                   q_ref, k_ref, v_ref, rw_ref, gate_ref):
    x = x_ref[...]
    ms = jnp.mean(x * x, axis=-1, keepdims=True)
    h = (x * lax.rsqrt(ms + NORM_EPS) * g_ref[...]).astype(BF16)

    def proj(c0, c1):
        return _dot(h, w_ref[:, c0:c1])

    rows = x.shape[0]
    lane = lax.broadcasted_iota(jnp.int32, (rows, DA_WIDTH), 1)
    first_half = (lane & (DA_HALF_DIM - 1)) < (DA_HALF_DIM // 2)

    reps = DA_WIDTH // LANES
    cos = jnp.concatenate([cos_ref[...]] * reps, axis=1)
    sin = jnp.concatenate([sin_ref[...]] * reps, axis=1)

    def rope(t):
        partner = jnp.where(first_half,
                            pltpu.roll(t, DA_WIDTH - DA_HALF_DIM // 2, 1),
                            pltpu.roll(t, DA_HALF_DIM // 2, 1))
        return t * cos + partner * sin

    q_ref[...] = (rope(proj(0, DA_WIDTH)) * Q_SCALE).astype(q_ref.dtype)
    k_ref[...] = rope(proj(DA_WIDTH, 2 * DA_WIDTH)).astype(k_ref.dtype)
    v_ref[...] = proj(2 * DA_WIDTH, 3 * DA_WIDTH).astype(v_ref.dtype)
    rw_ref[...] = proj(DA_IN_WIDTH, DA_IN_WIDTH + RW_IN_WIDTH)
    gate_ref[...] = proj(DA_IN_WIDTH + RW_IN_WIDTH, D_IN)


def _inproj(x2d, g, w_bf, cos, sin, seq):
    m = x2d.shape[0]
    tm = INPROJ_ROWS
    nseq = seq // tm
    row = lambda i: (i, 0)
    tab = lambda i: (i % nseq, 0)
    return pl.pallas_call(
        _inproj_kernel,
        grid=(m // tm,),
        in_specs=[
            pl.BlockSpec((tm, D_MODEL), row),
            _resident((1, D_MODEL)),
            _resident((D_MODEL, D_IN)),
            pl.BlockSpec((tm, LANES), tab),
            pl.BlockSpec((tm, LANES), tab),
        ],
        out_specs=[
            pl.BlockSpec((tm, DA_WIDTH), row),
            pl.BlockSpec((tm, DA_WIDTH), row),
            pl.BlockSpec((tm, DA_WIDTH), row),
            pl.BlockSpec((tm, RW_IN_WIDTH), row),
            pl.BlockSpec((tm, GATE_WIDTH), row),
        ],
        out_shape=[
            jax.ShapeDtypeStruct((m, DA_WIDTH), BF16),
            jax.ShapeDtypeStruct((m, DA_WIDTH), BF16),
            jax.ShapeDtypeStruct((m, DA_WIDTH), BF16),
            jax.ShapeDtypeStruct((m, RW_IN_WIDTH), F32),
            jax.ShapeDtypeStruct((m, GATE_WIDTH), F32),
        ],
        compiler_params=pltpu.CompilerParams(
            dimension_semantics=("parallel",), vmem_limit_bytes=VMEM_LIMIT_BYTES),
        name="inproj",
    )(x2d, g, w_bf, cos, sin)


def _attn_kernel(lq1_ref, lk1_ref, lq2_ref, lk2_ref, sg_ref, q_ref, k_ref, v_ref,
                 o_ref):
    nbatch, tq, width = q_ref.shape
    heads = width // DA_VDIM
    i = pl.program_id(1)
    lane = lax.broadcasted_iota(jnp.int32, (tq, DA_VDIM), 1)
    lam = (jnp.exp(jnp.sum(lq1_ref[...] * lk1_ref[...], axis=-1, keepdims=True))
           - jnp.exp(jnp.sum(lq2_ref[...] * lk2_ref[...], axis=-1, keepdims=True))
           + LAMBDA_INIT)
    r = lax.broadcasted_iota(jnp.int32, (2 * tq, tq), 0)
    c = lax.broadcasted_iota(jnp.int32, (2 * tq, tq), 1)
    allowed = (c // CHUNK) <= ((r & (tq - 1)) // CHUNK)

    def attend(nfull, bi, h):
        cols = slice(h * DA_VDIM, (h + 1) * DA_VDIM)
        q = q_ref[bi, :, cols]
        zero = jnp.zeros_like(q)
        qq = jnp.concatenate([jnp.where(lane < DA_HALF_DIM, q, zero),
                              jnp.where(lane >= DA_HALF_DIM, q, zero)], axis=0)
        s = []
        for j in range(nfull + 1):
            s_j = _dot_nt(qq, k_ref[bi, j * tq:(j + 1) * tq, cols])
            s.append(jnp.where(allowed, s_j, -jnp.inf) if j == nfull else s_j)
            yield
        m_lane = s[0]
        for s_j in s[1:]:
            m_lane = jnp.maximum(m_lane, s_j)
        m = jnp.max(m_lane, axis=-1, keepdims=True)
        l_lane = None
        acc = None
        for j, s_j in enumerate(s):
            p = jnp.exp2(s_j - m)
            pv = _dot(p.astype(BF16), v_ref[bi, j * tq:(j + 1) * tq, cols])
            l_lane = p if l_lane is None else l_lane + p
            acc = pv if acc is None else acc + pv
            if j < nfull:
                yield
        l = jnp.sum(l_lane, axis=-1, keepdims=True)
        o = acc / l
        od = o[:tq] - lam * o[tq:]
        ms = jnp.mean(od * od, axis=-1, keepdims=True)
        y = od * lax.rsqrt(ms + DA_SUBLN_EPS) * sg_ref[...]
        o_ref[bi, :, cols] = (y * (1.0 - LAMBDA_INIT)).astype(o_ref.dtype)

    def variant(nfull):
        gens = [attend(nfull, bi, h) for bi in range(nbatch) for h in range(heads)]
        nscore = nfull + 1
        for _ in range(nscore):
            next(gens[0])
        for k, cur in enumerate(gens):
            ahead = gens[k + 1] if k + 1 < len(gens) else None
            left = nscore if ahead is not None else 0
            for _ in cur:
                if left:
                    next(ahead)
                    left -= 1
            for _ in range(left):
                next(ahead)

    for n in range(k_ref.shape[1] // tq):
        pl.when(i == n)(functools.partial(variant, n))


def _attention(q, k, v, lq1, lk1, lq2, lk2, subln_g):
    b, s, _ = q.shape
    tq = ATTN_BLOCK
    assert tq & (tq - 1) == 0 and s % tq == 0
    vec = lambda n: pl.BlockSpec((1, n), lambda bi, qi: (0, 0))
    nb = ATTN_BATCH_GROUP
    assert b % nb == 0
    return pl.pallas_call(
        _attn_kernel,
        grid=(b // nb, s // tq),
        in_specs=[
            vec(DA_HALF_DIM), vec(DA_HALF_DIM), vec(DA_HALF_DIM), vec(DA_HALF_DIM),
            vec(DA_VDIM),
            pl.BlockSpec((nb, tq, DA_WIDTH), lambda bi, qi: (bi, qi, 0)),
            pl.BlockSpec((nb, s, DA_WIDTH), lambda bi, qi: (bi, 0, 0)),
            pl.BlockSpec((nb, s, DA_WIDTH), lambda bi, qi: (bi, 0, 0)),
        ],
        out_specs=pl.BlockSpec((nb, tq, DA_WIDTH), lambda bi, qi: (bi, qi, 0)),
        out_shape=jax.ShapeDtypeStruct((b, s, DA_WIDTH), BF16),
        compiler_params=pltpu.CompilerParams(
            dimension_semantics=("parallel", "arbitrary"),
            vmem_limit_bytes=VMEM_LIMIT_BYTES),
        name="diff_attn",
    )(lq1, lk1, lq2, lk2, subln_g, q, k, v)


_STAGED = ("r_t", "a_t", "k_t", "b_t", "k_h", "b_h", "v", "gam_c")


def _rwkv_stage(seq_start, rw_ref, prev_ref, mu_ref, w0_ref, a0_ref, kk_ref, ka_ref, rk_ref,
                wa2_ref, g2_ref, ones_ref, tri_ref, bonus_ref, g_ref, stage_ref):
    tt = rw_ref.shape[0]
    z = rw_ref[...]
    prev = jnp.where(seq_start, 0.0, prev_ref[7:8, :])
    row = lax.broadcasted_iota(jnp.int32, z.shape, 0)
    z_prev = jnp.where(row == 0, prev, pltpu.roll(z, 1, 0))
    zs = z + (z_prev - z) * mu_ref[...]
    yield

    w3 = 3 * RW_WIDTH
    r = zs[:, 0:RW_WIDTH]
    k = zs[:, RW_WIDTH:2 * RW_WIDTH]
    v = zs[:, 2 * RW_WIDTH:w3]
    x_wa = zs[:, w3:w3 + LANES]
    x_g = zs[:, w3 + LANES:w3 + 2 * LANES]
    lane = lax.broadcasted_iota(jnp.int32, x_wa.shape, 1)
    t_wa = jnp.where(lane < RW_DECAY_LORA, jnp.tanh(x_wa), x_wa)
    lora = _dot(t_wa.astype(BF16), wa2_ref[...])
    u = -(w0_ref[...] + lora[:, :RW_WIDTH])
    softplus = jnp.maximum(u, 0.0) + jnp.log(1.0 + jnp.exp(-jnp.abs(u)))
    log_decay = -jnp.exp(-softplus - 0.5)
    yield
    a_lr = jax.nn.sigmoid(a0_ref[...] + lora[:, RW_WIDTH:])
    g_ref[...] = _dot(jax.nn.sigmoid(x_g).astype(BF16), g2_ref[...]).astype(g_ref.dtype)
    yield

    ones_bd = ones_ref[...]
    kk = k * kk_ref[...]
    kk = kk * jnp.minimum(lax.rsqrt(_segsum(kk * kk, ones_bd)), 1e12)
    yield
    k2 = k * (1.0 + (a_lr - 1.0) * ka_ref[...])
    a_vec = -kk
    b_vec = kk * a_lr
    bonus_ref[...] = (_segsum(r * k2 * rk_ref[...], ones_bd) * v).astype(bonus_ref.dtype)
    yield

    tri = tri_ref[...]
    h1 = log_decay.astype(BF16)
    r1 = log_decay - h1.astype(F32)
    h2 = r1.astype(BF16)
    h3 = (r1 - h2.astype(F32)).astype(BF16)
    tw = tri.shape[0]
    cs = jnp.concatenate(
        [_dot(tri, h1[r0:r0 + tw]) + _dot(tri, h2[r0:r0 + tw]) + _dot(tri, h3[r0:r0 + tw])
         for r0 in range(0, tt, tw)], axis=0)
    nchunk = tt // CHUNK
    cl = jnp.concatenate(
        [jnp.broadcast_to(cs[(c + 1) * CHUNK - 1:(c + 1) * CHUNK, :], (CHUNK, RW_WIDTH))
         for c in range(nchunk)], axis=0)
    yield
    stage_ref[_STAGED.index("v")] = v
    stage_ref[_STAGED.index("gam_c")] = jnp.exp(cl)
    stage_ref[_STAGED.index("r_t")] = r * jnp.exp(cs)
    yield
    stage_ref[_STAGED.index("a_t")] = a_vec * jnp.exp(cs - log_decay)
    yield
    gam_inv = jnp.exp(-cs)
    stage_ref[_STAGED.index("k_t")] = k2 * gam_inv
    stage_ref[_STAGED.index("b_t")] = b_vec * gam_inv
    yield
    gam_end = jnp.exp(cl - cs)
    stage_ref[_STAGED.index("k_h")] = k2 * gam_end
    stage_ref[_STAGED.index("b_h")] = b_vec * gam_end


def _rwkv_chain(stage_ref, rm_ref, yg_ref):
    r_t, a_t, k_t, b_t, k_h, b_h, v, gam_c = _STAGED
    tt = stage_ref.shape[1]
    nchunk = tt // CHUNK

    quad = 2 * LANES
    cq = (CHUNK, quad)
    t_idx = lax.broadcasted_iota(jnp.int32, cq, 0)
    l_idx = lax.broadcasted_iota(jnp.int32, cq, 1)
    s_idx = l_idx & (RW_HEAD - 1)
    strict = s_idx < t_idx
    incl = s_idx <= t_idx
    diag = s_idx == t_idx
    even_head = (l_idx & RW_HEAD) == 0
    r4 = lax.broadcasted_iota(jnp.int32, (quad, quad), 0)
    l4 = lax.broadcasted_iota(jnp.int32, (quad, quad), 1)
    bd4_mask = (r4 // RW_HEAD) == (l4 // RW_HEAD)
    r2 = lax.broadcasted_iota(jnp.int32, (2 * CHUNK, LANES), 0)
    l2 = lax.broadcasted_iota(jnp.int32, (2 * CHUNK, LANES), 1)
    bd_mask = (r2 < CHUNK) == (l2 < RW_HEAD)
    eye = (r2 == l2).astype(F32)
    eye2 = jnp.concatenate([eye, eye], axis=0)
    z_pair = jnp.zeros((2 * CHUNK, LANES), F32)
    z_half = jnp.zeros((CHUNK, LANES), F32)

    def bd(x):
        return jnp.where(bd_mask, jnp.concatenate([x, x], axis=0), 0.0)

    def bd4(x):
        return jnp.where(bd4_mask, jnp.concatenate([x, x, x, x], axis=0), 0.0)

    def blockdiag2(xa, xb):
        zero = z_pair.astype(xa.dtype)
        return jnp.concatenate([jnp.concatenate([xa, zero], axis=1),
                                jnp.concatenate([zero, xb], axis=1)], axis=0)

    def stacked_bd(st):
        return blockdiag2(st[:2 * CHUNK], st[2 * CHUNK:])

    items = [(c, qd) for c in range(nchunk) for qd in range(RW_WIDTH // quad)]

    def view(name, item):
        c, qd = item
        return stage_ref[_STAGED.index(name), c * CHUNK:(c + 1) * CHUNK, qd * quad:(qd + 1) * quad]

    a2 = [_dot_nt(jnp.concatenate([view(a_t, it), view(r_t, it)], axis=0).astype(BF16),
                  jnp.concatenate([bd4(view(b_t, it)), bd4(view(k_t, it))], axis=0).astype(BF16))
          for it in items]
    yield
    a_ab = [jnp.where(strict, z[:CHUNK, :quad], 0.0) for z in a2]
    a_ak = [jnp.where(strict, z[:CHUNK, quad:], 0.0) for z in a2]
    a_rb = [jnp.where(incl, z[CHUNK:, :quad], 0.0) for z in a2]
    a_rk = [jnp.where(incl, z[CHUNK:, quad:], 0.0) for z in a2]

    n_st = [jnp.concatenate([bd(z[:, :LANES]), bd(z[:, LANES:])], axis=0) for z in a_ab]
    t_st = [eye2 + n for n in n_st]
    n_bf = [n.astype(BF16) for n in n_st]
    p_st = [_dot(stacked_bd(n), n) for n in n_bf]
    yield
    for _ in range(4):
        p_bf = [p.astype(BF16) for p in p_st]
        zz = [_dot(stacked_bd(p), jnp.concatenate([p, t.astype(BF16)], axis=1))
              for p, t in zip(p_bf, t_st)]
        p_st = [z[:, :LANES] for z in zz]
        t_st = [t + z[:, LANES:] for t, z in zip(t_st, zz)]
        yield
    t_st = [t + _dot(stacked_bd(p.astype(BF16)), t.astype(BF16)) for p, t in zip(p_st, t_st)]
    yield

    bd4_v = [bd4(view(v, it)) for it in items]
    av = [_dot(x.astype(BF16), bv.astype(BF16)) for x, bv in zip(a_ak, bd4_v)]
    x_st = [jnp.concatenate(
        [jnp.concatenate([bd(view(a_t, it)[:, :LANES]), bd(u[:, :LANES])], axis=1),
         jnp.concatenate([bd(view(a_t, it)[:, LANES:]), bd(u[:, LANES:])], axis=1)], axis=0)
        for it, u in zip(items, av)]
    tx = [_dot(stacked_bd(t).astype(BF16), x.astype(BF16)) for t, x in zip(t_st, x_st)]
    yield

    for n_item, (it, tx_i, arb, ark, bv4) in enumerate(zip(items, tx, a_rb, a_rk, bd4_v)):
        if n_item and n_item % 2 == 0:
            yield
        c, qd = it
        r_bar, y_bar, au = [], [], []
        for half in range(2):
            pl_ = slice(half * LANES, (half + 1) * LANES)
            rows = slice(half * 2 * CHUNK, (half + 1) * 2 * CHUNK)
            tx_p = tx_i[rows]
            rhs = jnp.concatenate(
                [tx_p, jnp.concatenate([z_pair, bv4[rows, pl_]], axis=1)], axis=0)
            ry = _dot(jnp.concatenate([arb[:, pl_], ark[:, pl_]], axis=1).astype(BF16),
                      rhs.astype(BF16))
            r_bar.append(view(r_t, it)[:, pl_] + ry[:, :LANES])
            y_bar.append(ry[:, LANES:])
            au.append(tx_p[:CHUNK] + tx_p[CHUNK:])
        v_q = view(v, it)
        mg_rhs = jnp.concatenate(
            [au[0], jnp.concatenate([z_half, v_q[:, :LANES]], axis=1),
             au[1], jnp.concatenate([z_half, v_q[:, LANES:]], axis=1)], axis=0)
        lhs_t = jnp.concatenate([view(b_h, it), view(k_h, it)], axis=0).T
        mg = _dot(stacked_bd(lhs_t).astype(BF16), mg_rhs.astype(BF16))
        m_sel = jnp.concatenate([mg[0:CHUNK], mg[2 * CHUNK:3 * CHUNK]], axis=1)
        m_odd = jnp.concatenate([mg[CHUNK:2 * CHUNK], mg[3 * CHUNK:]], axis=1)
        m_even_odd = jnp.where(jnp.concatenate([even_head, even_head], axis=1), m_sel, m_odd)
        m_quad = (jnp.concatenate([m_even_odd[:, :LANES], m_even_odd[:, 2 * LANES:3 * LANES]], axis=1)
                  + jnp.where(diag, view(gam_c, it), 0.0))
        g_quad = jnp.concatenate([m_even_odd[:, LANES:2 * LANES], m_even_odd[:, 3 * LANES:]], axis=1)
        cols = slice(qd * quad, (qd + 1) * quad)
        rm_ref[c, :, cols] = jnp.concatenate(
            [jnp.concatenate(r_bar, axis=1), m_quad], axis=0).astype(rm_ref.dtype)
        yg_ref[c, :, cols] = jnp.concatenate([jnp.concatenate(y_bar, axis=1), g_quad], axis=0)


def _rwkv_prep_kernel(rw_ref, prev_ref, mu_ref, w0_ref, a0_ref, kk_ref, ka_ref, rk_ref,
                      wa2_ref, g2_ref, ones_ref, tri_ref,
                      rm_ref, yg_ref, bonus_ref, g_ref, stage_a, stage_b,
                      *, ntiles, tiles_per_seq):
    t = pl.program_id(0)
    tile = jnp.minimum(t, ntiles - 1)
    seq_start = (tile % tiles_per_seq) == 0

    @pl.when(t == 0)
    def _():
        stage_b[...] = jnp.zeros(stage_b.shape, F32)

    def step(write_ref, read_ref):
        pending = [
            _rwkv_stage(seq_start, rw_ref, prev_ref, mu_ref, w0_ref, a0_ref, kk_ref, ka_ref,
                        rk_ref, wa2_ref, g2_ref, ones_ref, tri_ref, bonus_ref, g_ref, write_ref),
            _rwkv_chain(read_ref, rm_ref, yg_ref),
        ]
        while pending:
            for gen in list(pending):
                try:
                    next(gen)
                except StopIteration:
                    pending.remove(gen)

    pl.when(t % 2 == 0)(functools.partial(step, stage_a, stage_b))
    pl.when(t % 2 == 1)(functools.partial(step, stage_b, stage_a))


def _rwkv_prep(rw, mu, w0, a0, k_k, k_a, r_k, wa2, g2, ones_bd, tri, seq):
    m = rw.shape[0]
    tt = RW_PREP_ROWS
    cpt = tt // CHUNK
    ntiles = m // tt
    cur = lambda t: jnp.minimum(t, ntiles - 1)
    done = lambda t: jnp.maximum(t - 1, 0)
    vec = lambda n: pl.BlockSpec((1, n), lambda t: (0, 0))
    return pl.pallas_call(
        functools.partial(_rwkv_prep_kernel, ntiles=ntiles, tiles_per_seq=seq // tt),
        grid=(ntiles + 1,),
        in_specs=[
            pl.BlockSpec((tt, RW_IN_WIDTH), lambda t: (cur(t), 0)),
            pl.BlockSpec((8, RW_IN_WIDTH), lambda t: (jnp.maximum(cur(t) * (tt // 8) - 1, 0), 0)),
            vec(RW_IN_WIDTH), vec(RW_WIDTH), vec(RW_WIDTH), vec(RW_WIDTH), vec(RW_WIDTH),
            vec(RW_WIDTH),
            pl.BlockSpec((LANES, 2 * RW_WIDTH), lambda t: (0, 0)),
            pl.BlockSpec((RW_GATE_LORA, RW_WIDTH), lambda t: (0, 0)),
            pl.BlockSpec((SEG_TILE, SEG_TILE), lambda t: (0, 0)),
            pl.BlockSpec((MXU_TILE, MXU_TILE), lambda t: (0, 0)),
        ],
        out_specs=[
            pl.BlockSpec((cpt, 2 * CHUNK, RW_WIDTH), lambda t: (done(t), 0, 0)),
            pl.BlockSpec((cpt, 2 * CHUNK, RW_WIDTH), lambda t: (done(t), 0, 0)),
            pl.BlockSpec((tt, RW_WIDTH), lambda t: (cur(t), 0)),
            pl.BlockSpec((tt, RW_WIDTH), lambda t: (cur(t), 0)),
        ],
        out_shape=[
            jax.ShapeDtypeStruct((m // CHUNK, 2 * CHUNK, RW_WIDTH), BF16),
            jax.ShapeDtypeStruct((m // CHUNK, 2 * CHUNK, RW_WIDTH), F32),
            jax.ShapeDtypeStruct((m, RW_WIDTH), BF16),
            jax.ShapeDtypeStruct((m, RW_WIDTH), BF16),
        ],
        scratch_shapes=[
            pltpu.VMEM((len(_STAGED), tt, RW_WIDTH), F32),
            pltpu.VMEM((len(_STAGED), tt, RW_WIDTH), F32),
        ],
        compiler_params=pltpu.CompilerParams(
            dimension_semantics=("arbitrary",), vmem_limit_bytes=VMEM_LIMIT_BYTES),
        name="rwkv_prep",
    )(rw, rw, mu, w0, a0, k_k, k_a, r_k, wa2, g2, ones_bd, tri)


def _rwkv_scan_kernel(rm_ref, yg_ref, bonus_ref, g_ref, lng_ref, lnb_ref, ones_ref,
                      y_ref, h_sc, y_sc):
    step = pl.program_id(0)
    nb, nchunk = rm_ref.shape[0], rm_ref.shape[1]
    rows = nchunk * CHUNK

    @pl.when(step == 0)
    def _():
        h_sc[...] = jnp.zeros(h_sc.shape, F32)

    r128 = lax.broadcasted_iota(jnp.int32, (2 * CHUNK, LANES), 0)
    l128 = lax.broadcasted_iota(jnp.int32, (2 * CHUNK, LANES), 1)
    bd_mask = (r128 < CHUNK) == (l128 < RW_HEAD)

    for c in range(nchunk):
        for bi in range(nb):
            for p in range(RW_PAIRS):
                lanes = slice(p * LANES, (p + 1) * LANES)
                h_bd = h_sc[bi * RW_PAIRS + p]
                out = _dot(rm_ref[bi, c, :, lanes], h_bd.astype(BF16)) + yg_ref[bi, c, :, lanes]
                y_sc[bi, c * CHUNK:(c + 1) * CHUNK, lanes] = out[:CHUNK]
                h_new = out[CHUNK:]
                h_sc[bi * RW_PAIRS + p] = jnp.where(
                    bd_mask, jnp.concatenate([h_new, h_new], axis=0), 0.0)

    ones_bd = ones_ref[...]
    y = y_sc[...].reshape(nb * rows, RW_WIDTH)
    mean = _segsum(y, ones_bd) * (1.0 / RW_HEAD)
    d = y - mean
    var = _segsum(d * d, ones_bd) * (1.0 / RW_HEAD)
    yn = d * lax.rsqrt(var + RW_GN_EPS) * lng_ref[...] + lnb_ref[...]
    bonus = bonus_ref[...].astype(F32).reshape(nb * rows, RW_WIDTH)
    gate = g_ref[...].astype(F32).reshape(nb * rows, RW_WIDTH)
    y_ref[...] = ((yn + bonus) * gate).reshape(nb, rows, RW_WIDTH).astype(y_ref.dtype)


def _rwkv_scan(rm, yg, bonus, g, ln_g, ln_b, ones_bd):
    b, nc, _, _ = rm.shape
    s = nc * CHUNK
    vec = lambda n: pl.BlockSpec((1, n), lambda ci: (0, 0))
    cps = SCAN_CHUNKS
    rows = cps * CHUNK
    assert nc % cps == 0
    return pl.pallas_call(
        _rwkv_scan_kernel,
        grid=(nc // cps,),
        in_specs=[
            pl.BlockSpec((b, cps, 2 * CHUNK, RW_WIDTH), lambda ci: (0, ci, 0, 0)),
            pl.BlockSpec((b, cps, 2 * CHUNK, RW_WIDTH), lambda ci: (0, ci, 0, 0)),
            pl.BlockSpec((b, rows, RW_WIDTH), lambda ci: (0, ci, 0)),
            pl.BlockSpec((b, rows, RW_WIDTH), lambda ci: (0, ci, 0)),
            vec(RW_WIDTH), vec(RW_WIDTH),
            pl.BlockSpec((SEG_TILE, SEG_TILE), lambda ci: (0, 0)),
        ],
        out_specs=pl.BlockSpec((b, rows, RW_WIDTH), lambda ci: (0, ci, 0)),
        out_shape=jax.ShapeDtypeStruct((b, s, RW_WIDTH), BF16),
        scratch_shapes=[
            pltpu.VMEM((b * RW_PAIRS, 2 * CHUNK, LANES), F32),
            pltpu.VMEM((b, rows, RW_WIDTH), F32),
        ],
        compiler_params=pltpu.CompilerParams(
            dimension_semantics=("arbitrary",), vmem_limit_bytes=VMEM_LIMIT_BYTES),
        name="rwkv_scan",
    )(rm, yg, bonus, g, ln_g, ln_b, ones_bd)


def _merge_ffn_kernel(x_ref, ya_ref, yb_ref, gate_ref, wa_ref, wb_ref, wo_ref,
                      gffn_ref, w1_ref, w2_ref, gfin_ref, o_ref):
    pa = _dot(ya_ref[...], wa_ref[...])
    pb = _dot(yb_ref[...], wb_ref[...])
    gate = gate_ref[...]
    merged = (jax.nn.sigmoid(gate[:, :D_MODEL]) * pa
              + jax.nn.sigmoid(gate[:, D_MODEL:]) * pb)
    x1 = x_ref[...] + _dot(merged.astype(BF16), wo_ref[...])
    ms = jnp.mean(x1 * x1, axis=-1, keepdims=True)
    h = (x1 * lax.rsqrt(ms + NORM_EPS) * gffn_ref[...]).astype(BF16)
    acc = x1
    for c in range(D_FF // FFN_CHUNK):
        cols = slice(c * FFN_CHUNK, (c + 1) * FFN_CHUNK)
        f = jnp.maximum(_dot(h, w1_ref[:, cols]), 0.0)
        acc = acc + _dot((f * f).astype(BF16), w2_ref[cols, :])
    ms2 = jnp.mean(acc * acc, axis=-1, keepdims=True)
    o_ref[...] = acc * lax.rsqrt(ms2 + NORM_EPS) * gfin_ref[...]


def _merge_ffn(x2d, ya, yb, gates, wa, wb, wo, gffn, w1, w2, gfin):
    m = x2d.shape[0]
    tm = FFN_ROWS
    row = lambda i: (i, 0)
    return pl.pallas_call(
        _merge_ffn_kernel,
        grid=(m // tm,),
        in_specs=[
            pl.BlockSpec((tm, D_MODEL), row),
            pl.BlockSpec((tm, DA_WIDTH), row),
            pl.BlockSpec((tm, RW_WIDTH), row),
            pl.BlockSpec((tm, GATE_WIDTH), row),
            _resident((DA_WIDTH, D_MODEL)),
            _resident((RW_WIDTH, D_MODEL)),
            _resident((D_MODEL, D_MODEL)),
            _resident((1, D_MODEL)),
            _resident((D_MODEL, D_FF)),
            _resident((D_FF, D_MODEL)),
            _resident((1, D_MODEL)),
        ],
        out_specs=pl.BlockSpec((tm, D_MODEL), row),
        out_shape=jax.ShapeDtypeStruct((m, D_MODEL), F32),
        compiler_params=pltpu.CompilerParams(
            dimension_semantics=("parallel",), vmem_limit_bytes=VMEM_LIMIT_BYTES),
        name="merge_ffn",
    )(x2d, ya, yb, gates, wa, wb, wo, gffn, w1, w2, gfin)


def _rope_tables(seq):
    d = DA_HALF_DIM
    pos = jnp.arange(seq, dtype=F32)
    inv_freq = ROPE_THETA ** (-jnp.arange(0, d, 2, dtype=F32) / d)
    ang = pos[:, None] * inv_freq[None, :]
    cos = jnp.cos(ang)
    sin = jnp.sin(ang)
    reps = LANES // d
    cos_full = jnp.tile(jnp.concatenate([cos, cos], axis=-1), (1, reps))
    sin_full = jnp.tile(jnp.concatenate([-sin, sin], axis=-1), (1, reps))
    return cos_full, sin_full


def kernel(x, norm_mix_g, w_in, rw_mu, rw_w0, rw_w2, rw_a0, rw_a2, rw_g2, rw_k_k, rw_k_a,
           rw_r_k, rw_ln_g, rw_ln_b, da_lq1, da_lk1, da_lq2, da_lk2, da_subln_g,
           w_branch_a, w_branch_b, w_o, norm_ffn_g, w_ff1, w_ff2, norm_final_g):
    b, s, d = x.shape
    assert d == D_MODEL and norm_mix_g.shape[0] == 1
    assert s % ATTN_BLOCK == 0 and s % INPROJ_ROWS == 0 and s % RW_PREP_ROWS == 0
    assert (b * s) % FFN_ROWS == 0
    row = lambda t: t.reshape(1, -1)
    x2d = x.reshape(b * s, d)

    cos, sin = _rope_tables(s)
    q, k, v, rw, gates = _inproj(x2d, norm_mix_g, w_in[0].astype(BF16), cos, sin, s)

    ya = _attention(q.reshape(b, s, DA_WIDTH), k.reshape(b, s, DA_WIDTH),
                    v.reshape(b, s, DA_WIDTH), row(da_lq1), row(da_lk1), row(da_lq2),
                    row(da_lk2), row(da_subln_g))

    wa2 = jnp.zeros((LANES, 2 * RW_WIDTH), F32)
    wa2 = wa2.at[:RW_DECAY_LORA, :RW_WIDTH].set(rw_w2[0])
    wa2 = wa2.at[RW_DECAY_LORA:, RW_WIDTH:].set(rw_a2[0])
    idx = jnp.arange(SEG_TILE) // RW_HEAD
    ones_bd = (idx[:, None] == idx[None, :]).astype(BF16)
    tok = jnp.arange(MXU_TILE)
    tri = ((tok[:, None] >= tok[None, :])
           & (tok[:, None] // CHUNK == tok[None, :] // CHUNK)).astype(BF16)

    rm, yg, bonus, g = _rwkv_prep(
        rw, rw_mu, rw_w0, rw_a0, rw_k_k, rw_k_a, row(rw_r_k),
        wa2.astype(BF16), rw_g2[0].astype(BF16), ones_bd, tri, s)
    nc = s // CHUNK
    yb = _rwkv_scan(rm.reshape(b, nc, 2 * CHUNK, RW_WIDTH), yg.reshape(b, nc, 2 * CHUNK, RW_WIDTH),
                    bonus.reshape(b, s, RW_WIDTH), g.reshape(b, s, RW_WIDTH),
                    rw_ln_g, rw_ln_b, ones_bd)

    out = _merge_ffn(x2d, ya.reshape(b * s, DA_WIDTH), yb.reshape(b * s, RW_WIDTH), gates,
                     w_branch_a[0].astype(BF16), w_branch_b[0].astype(BF16),
                     w_o[0].astype(BF16), norm_ffn_g, w_ff1[0].astype(BF16),
                     w_ff2[0].astype(BF16), row(norm_final_g))
    return out.reshape(b, s, d)
```
